```python
import jax, jax.numpy as jnp
from jax import lax
import numpy as np

D_MODEL = 1024
BATCH = 32
SEQ = 256
DEPTH = 4
DEC_BATCH = 4
DEC_SEQ = 2048
PAST_LEN = 256

GRID_W = 64
HEAD_DIM = 64
GLOB_HEADS = 6
GLOB_KV = 2
RET_HEADS = 4
WIN_HEADS = 6
WIN_KV = 2
WINDOW = 128
BLOCK = 128
RET_CHUNK = 128
ROPE_BASE = 10000.0
QK_SCALE = HEAD_DIM ** -0.5
D_FF = -(-8 * D_MODEL // (3 * 256)) * 256
MIX_WIDTH = (GLOB_HEADS + RET_HEADS + WIN_HEADS) * HEAD_DIM
SPLIT_SIZES = (GLOB_HEADS * HEAD_DIM, GLOB_KV * HEAD_DIM, GLOB_KV * HEAD_DIM,
               RET_HEADS * HEAD_DIM, RET_HEADS * HEAD_DIM, RET_HEADS * HEAD_DIM, RET_HEADS * HEAD_DIM,
               WIN_HEADS * HEAD_DIM, WIN_KV * HEAD_DIM, WIN_KV * HEAD_DIM)
IN_WIDTH = sum(SPLIT_SIZES)
EPS = 1e-6

kernel_name = 'hymba_style_diffusion_retention_swa_step'

F32 = jnp.float32


def rms_norm(x, g):
    xf = x.astype(F32)
    y = xf * lax.rsqrt(jnp.mean(xf * xf, axis=-1, keepdims=True) + EPS)
    return (y * g.astype(F32)).astype(x.dtype)


def head_rms(x):
    xf = x.astype(F32)
    return xf * lax.rsqrt(jnp.mean(xf * xf, axis=-1, keepdims=True) + EPS)


def grid_positions(n):
    rows = n // GRID_W
    row = jnp.broadcast_to(jnp.arange(rows, dtype=F32)[:, None], (rows, GRID_W)).reshape(-1)
    col = jnp.broadcast_to(jnp.arange(GRID_W, dtype=F32)[None, :], (rows, GRID_W)).reshape(-1)
    return row, col


def rope_1d(x, pos):
    half = x.shape[-1] // 2
    freqs = ROPE_BASE ** (-jnp.arange(half, dtype=F32) / half)
    ang = pos[:, None] * freqs[None, :]
    cos = jnp.cos(ang)[None, :, None, :]
    sin = jnp.sin(ang)[None, :, None, :]
    xf = x.astype(F32)
    x1, x2 = xf[..., :half], xf[..., half:]
    return jnp.concatenate([x1 * cos - x2 * sin, x2 * cos + x1 * sin], axis=-1).astype(x.dtype)


def rope_2d(x, row, col):
    h = x.shape[-1] // 2
    return jnp.concatenate([rope_1d(x[..., :h], row), rope_1d(x[..., h:], col)], axis=-1)


def attn_probs(s, sink):
    s = s.astype(F32)
    if sink is None:
        return jax.nn.softmax(s, axis=-1)
    sk = jnp.broadcast_to(sink.astype(F32)[:, :, None, None], s.shape[:-1] + (1,))
    return jax.nn.softmax(jnp.concatenate([s, sk], axis=-1), axis=-1)[..., :-1]


def blocked_attention(q, k, v, sink=None):
    b, nq, h, d = q.shape
    hkv = k.shape[2]
    qb = q.reshape(b, nq // BLOCK, BLOCK, hkv, h // hkv, d).swapaxes(0, 1)

    def one_block(qblk):
        s = jnp.einsum('bqhgd,bkhd->bhgqk', qblk, k)
        p = attn_probs(s, sink).astype(v.dtype)
        return jnp.einsum('bhgqk,bkhd->bqhgd', p, v)

    out = lax.map(one_block, qb)
    return out.swapaxes(0, 1).reshape(b, nq, h, d)


def banded_window_attention(q, k, v, k_ctx, v_ctx, sink):
    b, n, h, d = q.shape
    hkv = k.shape[2]
    nb = n // BLOCK
    qb = q.reshape(b, nb, BLOCK, hkv, h // hkv, d)

    def neighbours(t):
        tb = t.reshape(b, nb, BLOCK, hkv, d)
        tp = jnp.pad(tb, ((0, 0), (1, 1), (0, 0), (0, 0), (0, 0)))
        return jnp.concatenate([tp[:, :-2], tp[:, 1:-1], tp[:, 2:]], axis=2)

    kn, vn = neighbours(k), neighbours(v)
    qi = jnp.arange(BLOCK)[:, None]
    kj = jnp.arange(3 * BLOCK)[None, :] - BLOCK
    within = jnp.abs(kj - qi) <= WINDOW
    key_pos = jnp.arange(nb)[:, None] * BLOCK + kj
    valid = (key_pos >= 0) & (key_pos < n)
    mask = within[None] & valid[:, None, :]
    s_loc = jnp.einsum('bnqhgd,bnkhd->bnhgqk', qb, kn).astype(F32)
    s_loc = jnp.where(mask[None, :, None, None], s_loc, -jnp.inf)
    s_ctx = jnp.einsum('bnqhgd,bkhd->bnhgqk', qb, k_ctx).astype(F32)
    p = attn_probs(jnp.concatenate([s_loc, s_ctx], axis=-1), sink).astype(v.dtype)
    nloc = 3 * BLOCK
    out = (jnp.einsum('bnhgqk,bnkhd->bnqhgd', p[..., :nloc], vn)
           + jnp.einsum('bnhgqk,bkhd->bnqhgd', p[..., nloc:], v_ctx))
    return out.reshape(b, n, h, d)


def retention_direction(q, k, v, log_gamma, s0):
    b, n, h, d = q.shape
    nc = n // RET_CHUNK

    def to_chunks(t):
        return t.reshape(b, nc, RET_CHUNK, h, d).swapaxes(0, 1)

    idx = jnp.arange(RET_CHUNK, dtype=F32)
    diff = idx[:, None] - idx[None, :]
    inner_decay = jnp.where(diff >= 0, jnp.exp(jnp.maximum(diff, 0.0)[None] * log_gamma[:, None, None]), 0.0)
    q_decay = jnp.exp((idx + 1.0)[:, None] * log_gamma[None, :])
    k_decay = jnp.exp((RET_CHUNK - 1.0 - idx)[:, None] * log_gamma[None, :])
    chunk_decay = jnp.exp(RET_CHUNK * log_gamma)

    def step(s, qkv):
        qc, kc, vc = qkv
        a = jnp.einsum('bnhd,bmhd->bhnm', qc, kc) * inner_decay
        o = (jnp.einsum('bhnm,bmhe->bnhe', a, vc)
             + jnp.einsum('bnhd,bhde->bnhe', qc, s) * q_decay[None, :, :, None])
        s = (s * chunk_decay[None, :, None, None]
             + jnp.einsum('bmhd,bmhe->bhde', kc * k_decay[None, :, :, None], vc))
        return s, o

    s_fin, out = lax.scan(step, s0, (to_chunks(q), to_chunks(k), to_chunks(v)))
    return out.swapaxes(0, 1).reshape(b, n, h, d), s_fin


def bidir_retention(q, k, v, lg_f, lg_b, s_f0, s_b0):
    o_f, s_f = retention_direction(q, k, v, lg_f, s_f0)
    o_b, s_b = retention_direction(q[:, ::-1], k[:, ::-1], v[:, ::-1], lg_b, s_b0)
    return o_f + o_b[:, ::-1], s_f, s_b


def modulation(cond, w_mod, b_mod):
    m = jnp.einsum('...d,de->...e', jax.nn.silu(cond), w_mod) + b_mod
    return jnp.split(jnp.expand_dims(m, -2), 6, axis=-1)


def project(h, w_in):
    p = jnp.einsum('bnd,de->bne', h, w_in)
    b, n, _ = p.shape
    offsets = []
    acc = 0
    for sz in SPLIT_SIZES[:-1]:
        acc += sz
        offsets.append(acc)
    return [t.reshape(b, n, -1, HEAD_DIM) for t in jnp.split(p, offsets, axis=-1)]


def trunk_layer(x, cond, lp, cache=None, pos=None):
    sh1, sc1, gt1, sh2, sc2, gt2 = modulation(cond, lp['w_mod'], lp['b_mod'])
    h = rms_norm(x, lp['g_pre_mix']) * (1 + sc1) + sh1
    gq, gk, gv, rq, rk, rv, rg, wq, wk, wv = project(h, lp['w_in'])
    gq = rms_norm(gq, lp['g_q'])
    gk = rms_norm(gk, lp['g_k'])
    sink = lp['win_sink'].reshape(WIN_KV, WIN_HEADS // WIN_KV)
    lg_f = jax.nn.log_sigmoid(lp['ret_decay_fwd'].astype(F32))
    lg_b = jax.nn.log_sigmoid(lp['ret_decay_bwd'].astype(F32))
    rq32 = rq.astype(F32)
    rk32 = rk.astype(F32) * QK_SCALE
    rv32 = rv.astype(F32)
    b = x.shape[0]
    if cache is None:
        zeros = jnp.zeros((b, RET_HEADS, HEAD_DIM, HEAD_DIM), F32)
        glob = blocked_attention(gq * QK_SCALE, gk, gv)
        ret, s_f, s_b = bidir_retention(rq32, rk32, rv32, lg_f, lg_b, zeros, zeros)
        win = blocked_attention(wq * QK_SCALE, wk, wv, sink)
        new = (gk, gv, s_f.astype(x.dtype), s_b.astype(x.dtype), wk, wv)
    else:
        ck, cv, sf0, sb0, cwk, cwv = cache
        row, col = pos
        k_all = jnp.concatenate([rope_2d(gk, row, col), ck], axis=1)
        v_all = jnp.concatenate([gv, cv], axis=1)
        glob = blocked_attention(rope_2d(gq, row, col) * QK_SCALE, k_all, v_all)
        ret, _, _ = bidir_retention(rq32, rk32, rv32, lg_f, lg_b, sf0.astype(F32), sb0.astype(F32))
        win = banded_window_attention(rope_2d(wq, row, col) * QK_SCALE, rope_2d(wk, row, col), wv,
                                      cwk, cwv, sink)
        new = None
    ret = (head_rms(ret) * jax.nn.silu(rg.astype(F32))).astype(x.dtype)
    n = x.shape[1]
    merged = jnp.concatenate([glob.reshape(b, n, -1), ret.reshape(b, n, -1), win.reshape(b, n, -1)], axis=-1)
    mixed = jnp.einsum('bne,ed->bnd', merged, lp['w_out'])
    x = x + gt1 * rms_norm(mixed, lp['g_post_mix'])
    h = rms_norm(x, lp['g_pre_ffn']) * (1 + sc2) + sh2
    a, g = jnp.split(jnp.einsum('bnd,df->bnf', h, lp['w_gate_up']), 2, axis=-1)
    ff = jnp.einsum('bnf,fd->bnd', jax.nn.silu(a) * g, lp['w_down'])
    x = x + gt2 * rms_norm(ff, lp['g_post_ffn'])
    return x, new


def setup_inputs(seed: int = 0) -> dict:
    key = jax.random.key(seed)
    ks = jax.random.split(key, 32)
    nrm = jax.random.normal
    ret_init = jnp.log(2.0 ** (5.0 + jnp.arange(RET_HEADS, dtype=F32)) - 1.0)
    return {
        'x_prompt': nrm(ks[0], (BATCH, SEQ, D_MODEL), F32),
        'x_sample': nrm(ks[1], (DEC_BATCH, DEC_SEQ, D_MODEL), F32),
        'cache_glob_k': nrm(ks[2], (DEC_BATCH, DEPTH, PAST_LEN, GLOB_KV, HEAD_DIM), F32),
        'cache_glob_v': nrm(ks[3], (DEC_BATCH, DEPTH, PAST_LEN, GLOB_KV, HEAD_DIM), F32),
        'state_ret_fwd': 0.5 * nrm(ks[4], (DEC_BATCH, DEPTH, RET_HEADS, HEAD_DIM, HEAD_DIM), F32),
        'state_ret_bwd': 0.5 * nrm(ks[5], (DEC_BATCH, DEPTH, RET_HEADS, HEAD_DIM, HEAD_DIM), F32),
        'cache_win_k': nrm(ks[6], (DEC_BATCH, DEPTH, PAST_LEN, WIN_KV, HEAD_DIM), F32),
        'cache_win_v': nrm(ks[7], (DEC_BATCH, DEPTH, PAST_LEN, WIN_KV, HEAD_DIM), F32),
        'c': nrm(ks[8], (DEC_BATCH, D_MODEL), F32),
        'c_ctx': nrm(ks[9], (D_MODEL,), F32),
        'w_mod': 0.5 * D_MODEL ** -0.5 * nrm(ks[10], (DEPTH, D_MODEL, 6 * D_MODEL), F32),
        'b_mod': 0.01 * nrm(ks[11], (DEPTH, 6 * D_MODEL), F32),
        'g_pre_mix': 1.0 + 0.05 * nrm(ks[12], (DEPTH, D_MODEL), F32),
        'g_post_mix': 1.0 + 0.05 * nrm(ks[13], (DEPTH, D_MODEL), F32),
        'g_pre_ffn': 1.0 + 0.05 * nrm(ks[14], (DEPTH, D_MODEL), F32),
        'g_post_ffn': 1.0 + 0.05 * nrm(ks[15], (DEPTH, D_MODEL), F32),
        'w_in': D_MODEL ** -0.5 * nrm(ks[16], (DEPTH, D_MODEL, IN_WIDTH), F32),
        'g_q': 1.0 + 0.05 * nrm(ks[17], (DEPTH, HEAD_DIM), F32),
        'g_k': 1.0 + 0.05 * nrm(ks[18], (DEPTH, HEAD_DIM), F32),
        'ret_decay_fwd': ret_init[None, :] + 0.1 * nrm(ks[19], (DEPTH, RET_HEADS), F32),
        'ret_decay_bwd': ret_init[None, :] + 0.1 * nrm(ks[20], (DEPTH, RET_HEADS), F32),
        'win_sink': 0.5 * nrm(ks[21], (DEPTH, WIN_HEADS), F32),
        'w_out': MIX_WIDTH ** -0.5 * nrm(ks[22], (DEPTH, MIX_WIDTH, D_MODEL), F32),
        'w_gate_up': D_MODEL ** -0.5 * nrm(ks[23], (DEPTH, D_MODEL, 2 * D_FF), F32),
        'w_down': D_FF ** -0.5 * nrm(ks[24], (DEPTH, D_FF, D_MODEL), F32),
    }


def reference(x_prompt, x_sample, cache_glob_k, cache_glob_v, state_ret_fwd, state_ret_bwd,
              cache_win_k, cache_win_v, c, c_ctx, w_mod, b_mod, g_pre_mix, g_post_mix,
              g_pre_ffn, g_post_ffn, w_in, g_q, g_k, ret_decay_fwd, ret_decay_bwd, win_sink,
              w_out, w_gate_up, w_down):
    def layer_params(l):
        return {'w_mod': w_mod[l], 'b_mod': b_mod[l], 'g_pre_mix': g_pre_mix[l],
                'g_post_mix': g_post_mix[l], 'g_pre_ffn': g_pre_ffn[l], 'g_post_ffn': g_post_ffn[l],
                'w_in': w_in[l], 'g_q': g_q[l], 'g_k': g_k[l], 'ret_decay_fwd': ret_decay_fwd[l],
                'ret_decay_bwd': ret_decay_bwd[l], 'win_sink': win_sink[l], 'w_out': w_out[l],
                'w_gate_up': w_gate_up[l], 'w_down': w_down[l]}

    y = x_prompt
    gk_l, gv_l, sf_l, sb_l, wk_l, wv_l = [], [], [], [], [], []
    for l in range(DEPTH):
        y, (gk, gv, sf, sb, wk, wv) = trunk_layer(y, c_ctx, layer_params(l))
        gk_l.append(gk); gv_l.append(gv); sf_l.append(sf)
        sb_l.append(sb); wk_l.append(wk); wv_l.append(wv)
    new_glob_k = jnp.stack(gk_l, axis=1)
    new_glob_v = jnp.stack(gv_l, axis=1)
    new_ret_fwd = jnp.stack(sf_l, axis=1)
    new_ret_bwd = jnp.stack(sb_l, axis=1)
    new_win_k = jnp.stack(wk_l, axis=1)
    new_win_v = jnp.stack(wv_l, axis=1)

    pos = grid_positions(x_sample.shape[1])
    z = x_sample
    for l in range(DEPTH):
        cache = (cache_glob_k[:, l], cache_glob_v[:, l], state_ret_fwd[:, l], state_ret_bwd[:, l],
                 cache_win_k[:, l], cache_win_v[:, l])
        z, _ = trunk_layer(z, c, layer_params(l), cache=cache, pos=pos)

    return (y, z, new_glob_k, new_glob_v, new_ret_fwd, new_ret_bwd, new_win_k, new_win_v)
```

```python
import functools

import jax
import jax.numpy as jnp
from jax import lax
from jax.experimental import pallas as pl
from jax.experimental.pallas import tpu as pltpu

F32 = jnp.float32
BF16 = jnp.bfloat16

D_MODEL = 1024
DEPTH = 4
GRID_W = 64
HEAD_DIM = 64
LANES = 128
GLOB_HEADS = 6
RET_HEADS = 4
WIN_HEADS = 6
KV_HEADS = 2
GROUP = GLOB_HEADS // KV_HEADS
WINDOW = 128
ROPE_BASE = 10000.0
QK_SCALE = HEAD_DIM ** -0.5
D_FF = 2816
IN_WIDTH = 2304
EPS = 1e-6
NEG = -1e30
MOD_ROWS = 8
CTX_ROW = 4

GQ0, GK0, GV0 = 0, 384, 512
RET0 = 640
WQ0, WK0, WV0 = 1664, 2048, 2176

VMEM_LIMIT = 48 * 1024 * 1024

TM_IN = 512
TM_OUT = 512
TM_FFN = 512
TF_FFN = 1408
RET_CHUNK = 256


def _cparams(sem):
    return pltpu.CompilerParams(dimension_semantics=sem, vmem_limit_bytes=VMEM_LIMIT)


def _dot(a, b):
    return jnp.dot(a, b, preferred_element_type=F32)


def _dot_nt(a, b):
    return lax.dot_general(a, b, (((1,), (1,)), ((), ())), preferred_element_type=F32)


def _dot_tn(a, b):
    return lax.dot_general(a, b, (((0,), (0,)), ((), ())), preferred_element_type=F32)


def _silu(x):
    return x / (1.0 + jnp.exp(-x))


def _row_rms(x):
    return x * lax.rsqrt(jnp.mean(x * x, axis=-1, keepdims=True) + EPS)


def _mod_kernel(c_ref, w_ref, b_ref, o_ref):
    s = _silu(c_ref[...]).astype(BF16)
    o_ref[0] = _dot(s, w_ref[0].astype(BF16)) + b_ref[0]


def _modulation(cond8, w_mod, b_mod):
    tn = 1536
    n_out = 6 * D_MODEL
    return pl.pallas_call(
        _mod_kernel,
        grid=(DEPTH, n_out // tn),
        in_specs=[pl.BlockSpec((MOD_ROWS, D_MODEL), lambda l, j: (0, 0)),
                  pl.BlockSpec((1, D_MODEL, tn), lambda l, j: (l, 0, j)),
                  pl.BlockSpec((1, 1, tn), lambda l, j: (l, 0, j))],
        out_specs=pl.BlockSpec((1, MOD_ROWS, tn), lambda l, j: (l, 0, j)),
        out_shape=jax.ShapeDtypeStruct((DEPTH, MOD_ROWS, n_out), F32),
        compiler_params=_cparams(("parallel", "parallel")),
        name="modulation",
    )(cond8, w_mod, b_mod.reshape(DEPTH, 1, n_out))


def _in_proj_kernel(*refs, rope, ctx):
    x_ref, mod_ref, gpre_ref, w_ref, gq_ref, gk_ref, bd_ref = refs[:7]
    pos = 7
    if rope:
        cos_ref, sa_ref, sb_ref = refs[pos:pos + 3]
        pos += 3
    qg_ref, kg_ref, vg_ref, r_ref, qw_ref, kw_ref, vw_ref = refs[pos:pos + 7]
    pos += 7
    if ctx:
        kg32_ref, vg32_ref, kw32_ref, vw32_ref = refs[pos:pos + 4]

    x = x_ref[...]
    m = mod_ref[0]
    sh1 = m[:, 0:D_MODEL]
    sc1 = m[:, D_MODEL:2 * D_MODEL]
    h = _row_rms(x) * (gpre_ref[...] * (1.0 + sc1)) + sh1
    hb = h.astype(BF16)
    tm = x.shape[0]
    lane_lo = lax.broadcasted_iota(jnp.int32, (tm, LANES), 1) < HEAD_DIM

    def head_norm(t, g):
        ss = _dot((t * t).astype(BF16), bd_ref[...])
        return t * lax.rsqrt(ss + EPS) * g

    def rot(t):
        if not rope:
            return t
        return (t * cos_ref[...] + pltpu.roll(t, 16, 1) * sa_ref[...]
                + pltpu.roll(t, LANES - 16, 1) * sb_ref[...])

    def store_q(slabs, q_ref):
        for hd in range(GLOB_HEADS):
            t = slabs[hd // 2]
            kv = hd // GROUP
            if hd % 2 != kv:
                t = pltpu.roll(t, HEAD_DIM, 1)
            keep = lane_lo if kv == 0 else jnp.logical_not(lane_lo)
            q_ref[:, hd * LANES:(hd + 1) * LANES] = jnp.where(keep, t * QK_SCALE, 0.0).astype(BF16)

    pg = _dot(hb, w_ref[:, GQ0:RET0])
    store_q([rot(head_norm(pg[:, s * LANES:(s + 1) * LANES], gq_ref[...])) for s in range(3)], qg_ref)
    kg = head_norm(pg[:, GK0:GK0 + LANES], gk_ref[...])
    vg = pg[:, GV0:GV0 + LANES]
    if ctx:
        kg32_ref[...] = kg
        vg32_ref[...] = vg
    kg_ref[...] = rot(kg).astype(BF16)
    vg_ref[...] = vg.astype(BF16)

    pr = _dot(hb, w_ref[:, RET0:WQ0])
    r_ref[:, 0:256] = pr[:, 0:256].astype(BF16)
    r_ref[:, 256:512] = (pr[:, 256:512] * QK_SCALE).astype(BF16)
    r_ref[:, 512:1024] = pr[:, 512:1024].astype(BF16)

    pw = _dot(hb, w_ref[:, WQ0:IN_WIDTH])
    store_q([rot(pw[:, s * LANES:(s + 1) * LANES]) for s in range(3)], qw_ref)
    kw = pw[:, WK0 - WQ0:WK0 - WQ0 + LANES]
    vw = pw[:, WV0 - WQ0:WV0 - WQ0 + LANES]
    if ctx:
        kw32_ref[...] = kw
        vw32_ref[...] = vw
    kw_ref[...] = rot(kw).astype(BF16)
    vw_ref[...] = vw.astype(BF16)


def _in_proj(x2d, mod_l, gpre, w_in_b, gq_t, gk_t, bd, rope_tabs, *, ctx, rows_per_batch):
    t_rows = x2d.shape[0]
    tm = TM_IN
    tiles_per_batch = rows_per_batch // tm
    rope = rope_tabs is not None
    if ctx:
        mod_idx = lambda i: (CTX_ROW, 0, 0)
    else:
        mod_idx = lambda i: (i // tiles_per_batch, 0, 0)
    const = lambda i: (0, 0)
    row = lambda i: (i, 0)
    in_specs = [pl.BlockSpec((tm, D_MODEL), row),
                pl.BlockSpec((1, 1, 6 * D_MODEL), mod_idx),
                pl.BlockSpec((1, D_MODEL), const),
                pl.BlockSpec((D_MODEL, IN_WIDTH), const),
                pl.BlockSpec((1, LANES), const),
                pl.BlockSpec((1, LANES), const),
                pl.BlockSpec((LANES, LANES), const)]
    args = [x2d, mod_l, gpre, w_in_b, gq_t, gk_t, bd]
    if rope:
        tab = pl.BlockSpec((tm, LANES), lambda i: (i % tiles_per_batch, 0))
        in_specs += [tab, tab, tab]
        args += list(rope_tabs)
    widths = [GLOB_HEADS * LANES, LANES, LANES, 4 * RET_HEADS * HEAD_DIM, WIN_HEADS * LANES, LANES, LANES]
    out_shape = [jax.ShapeDtypeStruct((t_rows, w), BF16) for w in widths]
    out_specs = [pl.BlockSpec((tm, w), row) for w in widths]
    if ctx:
        out_shape += [jax.ShapeDtypeStruct((t_rows, LANES), F32)] * 4
        out_specs += [pl.BlockSpec((tm, LANES), row)] * 4
    return pl.pallas_call(
        functools.partial(_in_proj_kernel, rope=rope, ctx=ctx),
        grid=(t_rows // tm,),
        in_specs=in_specs, out_specs=out_specs, out_shape=out_shape,
        compiler_params=_cparams(("parallel",)),
        name="in_proj_ctx" if ctx else "in_proj_smp",
    )(*args)


def _attn_kernel(*refs, kinds, has_sink, tq, n_lat):
    q_ref = refs[0]
    pos = 1
    srcs = []
    for kind in kinds:
        srcs.append((kind, refs[pos], refs[pos + 1]))
        pos += 2
    if has_sink:
        sink_ref = refs[pos]
        pos += 1
    o_ref = refs[pos]
    j = pl.program_id(1)
    rows = GROUP * tq
    lane_lo = lax.broadcasted_iota(jnp.int32, (tq, LANES), 1) < HEAD_DIM
    row_id = lax.broadcasted_iota(jnp.int32, (rows, 1), 0)

    heads = []
    for kv in range(KV_HEADS):
        qs = jnp.concatenate(
            [q_ref[0, :, (GROUP * kv + g) * LANES:(GROUP * kv + g + 1) * LANES] for g in range(GROUP)], axis=0)
        s_parts, v_parts = [], []
        for kind, k_ref, v_ref in srcs:
            if kind == "full":
                k = k_ref[0].astype(BF16)
                v = v_ref[0].astype(BF16)
                s = _dot_nt(qs, k)
            else:
                width = tq + 2 * WINDOW
                start = pl.multiple_of(jnp.clip(j * tq - WINDOW, 0, n_lat - width), LANES)
                k = k_ref[0, pl.ds(start, width), :]
                v = v_ref[0, pl.ds(start, width), :]
                s = _dot_nt(qs, k)
                qpos = j * tq + (lax.broadcasted_iota(jnp.int32, (rows, width), 0) & (tq - 1))
                kpos = start + lax.broadcasted_iota(jnp.int32, (rows, width), 1)
                s = jnp.where(jnp.abs(kpos - qpos) <= WINDOW, s, NEG)
            s_parts.append(s)
            v_parts.append(v)
        m = s_parts[0].max(axis=-1, keepdims=True)
        for s in s_parts[1:]:
            m = jnp.maximum(m, s.max(axis=-1, keepdims=True))
        if has_sink:
            sk = jnp.where(row_id < tq, sink_ref[GROUP * kv:GROUP * kv + 1, 0:1],
                           jnp.where(row_id < 2 * tq, sink_ref[GROUP * kv + 1:GROUP * kv + 2, 0:1],
                                     sink_ref[GROUP * kv + 2:GROUP * kv + 3, 0:1]))
            m = jnp.maximum(m, sk)
        den = jnp.exp(sk - m) if has_sink else jnp.zeros_like(m)
        acc = jnp.zeros((rows, LANES), F32)
        for s, v in zip(s_parts, v_parts):
            p = jnp.exp(s - m)
            den = den + p.sum(axis=-1, keepdims=True)
            acc = acc + _dot(p.astype(BF16), v)
        o = acc / den
        for g in range(GROUP):
            heads.append(o[g * tq:(g + 1) * tq])

    for slab in range(GLOB_HEADS // 2):
        a, b = heads[2 * slab], heads[2 * slab + 1]
        if (2 * slab) // GROUP != 0:
            a = pltpu.roll(a, HEAD_DIM, 1)
        if (2 * slab + 1) // GROUP != 1:
            b = pltpu.roll(b, HEAD_DIM, 1)
        o_ref[0, :, slab * LANES:(slab + 1) * LANES] = jnp.where(lane_lo, a, b).astype(BF16)


def _attention(q, srcs, kinds, sink_tab, *, tq, name):
    bsz, n, qw = q.shape
    assert tq & (tq - 1) == 0 and n % tq == 0
    in_specs = [pl.BlockSpec((1, tq, qw), lambda b, j: (b, j, 0))]
    args = [q]
    for k, v in srcs:
        spec = pl.BlockSpec((1, k.shape[1], LANES), lambda b, j: (b, 0, 0))
        in_specs += [spec, spec]
        args += [k, v]
    if sink_tab is not None:
        in_specs.append(pl.BlockSpec((8, LANES), lambda b, j: (0, 0)))
        args.append(sink_tab)
    ow = GLOB_HEADS * HEAD_DIM
    return pl.pallas_call(
        functools.partial(_attn_kernel, kinds=kinds, has_sink=sink_tab is not None, tq=tq, n_lat=n),
        grid=(bsz, n // tq),
        in_specs=in_specs,
        out_specs=pl.BlockSpec((1, tq, ow), lambda b, j: (b, j, 0)),
        out_shape=jax.ShapeDtypeStruct((bsz, n, ow), BF16),
        compiler_params=_cparams(("parallel", "parallel")),
        name=name,
    )(*args)


def _ret_kernel(*refs, n, chunk, has_init, emit_state):
    r_ref, dec_ref, bd_ref = refs[:3]
    pos = 3
    if has_init:
        s0f_ref, s0b_ref = refs[pos:pos + 2]
        pos += 2
    o_ref = refs[pos]
    pos += 1
    if emit_state:
        sf_ref, sb_ref = refs[pos:pos + 2]
        pos += 2
    stf, stb, dsum, dec = refs[pos:pos + 4]
    nc = n // chunk
    n_slab = RET_HEADS // 2
    use_inter = has_init or nc > 1

    x = dec_ref[...]
    lg = jnp.minimum(x, 0.0) - jnp.log(1.0 + jnp.exp(-jnp.abs(x)))
    rowc = lax.broadcasted_iota(jnp.int32, (chunk, LANES), 0).astype(F32)
    lane_lo = lax.broadcasted_iota(jnp.int32, (chunk, LANES), 1) < HEAD_DIM
    row_s = lax.broadcasted_iota(jnp.int32, (LANES, LANES), 0) < HEAD_DIM
    lane_s = lax.broadcasted_iota(jnp.int32, (LANES, LANES), 1) < HEAD_DIM
    blockdiag = row_s == lane_s
    diff = (lax.broadcasted_iota(jnp.int32, (chunk, chunk), 0)
            - lax.broadcasted_iota(jnp.int32, (chunk, chunk), 1)).astype(F32)

    for hd in range(RET_HEADS):
        lf = lg[hd:hd + 1, 0:1]
        lb = lg[RET_HEADS + hd:RET_HEADS + hd + 1, 0:1]
        dsum[hd] = (jnp.where(diff >= 0, jnp.exp(jnp.maximum(diff, 0.0) * lf), 0.0)
                    + jnp.where(diff <= 0, jnp.exp(jnp.maximum(-diff, 0.0) * lb), 0.0))
    chunk_decay_f, chunk_decay_b = [], []
    for sl in range(n_slab):
        lf2 = jnp.where(lane_lo, lg[2 * sl:2 * sl + 1], lg[2 * sl + 1:2 * sl + 2])
        lb2 = jnp.where(lane_lo, lg[RET_HEADS + 2 * sl:RET_HEADS + 2 * sl + 1],
                        lg[RET_HEADS + 2 * sl + 1:RET_HEADS + 2 * sl + 2])
        dec[0 + sl] = jnp.exp((chunk - 1.0 - rowc) * lf2)
        dec[2 + sl] = jnp.exp(rowc * lb2)
        dec[4 + sl] = jnp.exp((rowc + 1.0) * lf2)
        dec[6 + sl] = jnp.exp((chunk - rowc) * lb2)
        chunk_decay_f.append(jnp.exp(chunk * jnp.where(row_s, lg[2 * sl:2 * sl + 1], lg[2 * sl + 1:2 * sl + 2])))
        chunk_decay_b.append(jnp.exp(chunk * jnp.where(row_s, lg[RET_HEADS + 2 * sl:RET_HEADS + 2 * sl + 1],
                                                       lg[RET_HEADS + 2 * sl + 1:RET_HEADS + 2 * sl + 2])))

    for sl in range(n_slab):
        stf[0, sl] = s0f_ref[0, sl] if has_init else jnp.zeros((LANES, LANES), F32)
        stb[nc, sl] = s0b_ref[0, sl] if has_init else jnp.zeros((LANES, LANES), F32)

    def local_state(ci, carry):
        base = pl.multiple_of(ci * chunk, chunk)
        for sl in range(n_slab):
            k2 = r_ref[0, pl.ds(base, chunk), 256 + sl * LANES:256 + (sl + 1) * LANES].astype(F32)
            v2 = r_ref[0, pl.ds(base, chunk), 512 + sl * LANES:512 + (sl + 1) * LANES]
            kk = jnp.concatenate([(k2 * dec[0 + sl]).astype(BF16), (k2 * dec[2 + sl]).astype(BF16)], axis=1)
            u = _dot_tn(kk, v2)
            stf[ci + 1, sl] = jnp.where(blockdiag, u[:LANES], 0.0)
            stb[ci, sl] = jnp.where(blockdiag, u[LANES:], 0.0)
        return carry

    lax.fori_loop(0, nc, local_state, 0)
    for sl in range(n_slab):
        for ci in range(nc):
            stf[ci + 1, sl] = stf[ci, sl] * chunk_decay_f[sl] + stf[ci + 1, sl]
        for ci in range(nc - 1, -1, -1):
            stb[ci, sl] = stb[ci + 1, sl] * chunk_decay_b[sl] + stb[ci, sl]
        if emit_state:
            sf_ref[0, sl] = stf[nc, sl]
            sb_ref[0, sl] = stb[0, sl]

    def outputs(ci, carry):
        base = pl.multiple_of(ci * chunk, chunk)
        for sl in range(n_slab):
            cols = lambda off: slice(off + sl * LANES, off + (sl + 1) * LANES)
            q2 = r_ref[0, pl.ds(base, chunk), cols(0)]
            k2 = r_ref[0, pl.ds(base, chunk), cols(256)]
            v2 = r_ref[0, pl.ds(base, chunk), cols(512)]
            g2 = r_ref[0, pl.ds(base, chunk), cols(768)].astype(F32)
            zero = jnp.zeros_like(q2)
            a_lo = (_dot_nt(jnp.where(lane_lo, q2, zero), k2) * dsum[2 * sl]).astype(BF16)
            a_hi = (_dot_nt(jnp.where(lane_lo, zero, q2), k2) * dsum[2 * sl + 1]).astype(BF16)
            o2 = jnp.where(lane_lo, _dot(a_lo, v2), _dot(a_hi, v2))
            if use_inter:
                st = jnp.concatenate([stf[ci, sl], stb[ci + 1, sl]], axis=1).astype(BF16)
                it = _dot(q2, st)
                o2 = o2 + it[:, :LANES] * dec[4 + sl] + it[:, LANES:] * dec[6 + sl]
            ss = _dot((o2 * o2).astype(BF16), bd_ref[...])
            y = o2 * lax.rsqrt(ss + EPS) * _silu(g2)
            o_ref[0, pl.ds(base, chunk), sl * LANES:(sl + 1) * LANES] = y.astype(BF16)
        return carry

    lax.fori_loop(0, nc, outputs, 0)


def _retention(r, dec_tab, bd, init, *, emit_state, name):
    bsz, n, rw = r.shape
    chunk = RET_CHUNK
    nc = n // chunk
    n_slab = RET_HEADS // 2
    has_init = init is not None
    st_spec = pl.BlockSpec((1, n_slab, LANES, LANES), lambda b: (b, 0, 0, 0))
    in_specs = [pl.BlockSpec((1, n, rw), lambda b: (b, 0, 0)),
                pl.BlockSpec((8, LANES), lambda b: (0, 0)),
                pl.BlockSpec((LANES, LANES), lambda b: (0, 0))]
    args = [r, dec_tab, bd]
    if has_init:
        in_specs += [st_spec, st_spec]
        args += list(init)
    ow = RET_HEADS * HEAD_DIM
    out_shape = [jax.ShapeDtypeStruct((bsz, n, ow), BF16)]
    out_specs = [pl.BlockSpec((1, n, ow), lambda b: (b, 0, 0))]
    if emit_state:
        out_shape += [jax.ShapeDtypeStruct((bsz, n_slab, LANES, LANES), F32)] * 2
        out_specs += [st_spec, st_spec]
    return pl.pallas_call(
        functools.partial(_ret_kernel, n=n, chunk=chunk, has_init=has_init, emit_state=emit_state),
        grid=(bsz,),
        in_specs=in_specs, out_specs=out_specs, out_shape=out_shape,
        scratch_shapes=[pltpu.VMEM((nc + 1, n_slab, LANES, LANES), F32),
                        pltpu.VMEM((nc + 1, n_slab, LANES, LANES), F32),
                        pltpu.VMEM((RET_HEADS, chunk, chunk), F32),
                        pltpu.VMEM((8, chunk, LANES), F32)],
        compiler_params=_cparams(("parallel",)),
        name=name,
    )(*args)


def _out_proj_kernel(x_ref, og_ref, or_ref, ow_ref, mod_ref, w_ref, gpost_ref, gpre2_ref, x1_ref, h2_ref):
    merged = jnp.concatenate([og_ref[...], or_ref[...], ow_ref[...]], axis=1)
    mixed = _dot(merged, w_ref[...])
    m = mod_ref[0]
    gt1 = m[:, 2 * D_MODEL:3 * D_MODEL]
    sh2 = m[:, 3 * D_MODEL:4 * D_MODEL]
    sc2 = m[:, 4 * D_MODEL:5 * D_MODEL]
    x1 = x_ref[...] + gt1 * (_row_rms(mixed) * gpost_ref[...])
    x1_ref[...] = x1
    h2_ref[...] = (_row_rms(x1) * (gpre2_ref[...] * (1.0 + sc2)) + sh2).astype(BF16)


def _mod_index(ctx, tiles_per_batch):
    if ctx:
        return lambda i, *_: (CTX_ROW, 0, 0)
    return lambda i, *_: (i // tiles_per_batch, 0, 0)


def _out_proj(x2d, og, orr, ow, mod_l, w_out_b, gpost, gpre2, *, ctx, rows_per_batch):
    t_rows = x2d.shape[0]
    tm = TM_OUT
    row = lambda i: (i, 0)
    const = lambda i: (0, 0)
    return pl.pallas_call(
        _out_proj_kernel,
        grid=(t_rows // tm,),
        in_specs=[pl.BlockSpec((tm, D_MODEL), row),
                  pl.BlockSpec((tm, og.shape[1]), row),
                  pl.BlockSpec((tm, orr.shape[1]), row),
                  pl.BlockSpec((tm, ow.shape[1]), row),
                  pl.BlockSpec((1, 1, 6 * D_MODEL), _mod_index(ctx, rows_per_batch // tm)),
                  pl.BlockSpec((D_MODEL, D_MODEL), const),
                  pl.BlockSpec((1, D_MODEL), const),
                  pl.BlockSpec((1, D_MODEL), const)],
        out_specs=[pl.BlockSpec((tm, D_MODEL), row), pl.BlockSpec((tm, D_MODEL), row)],
        out_shape=[jax.ShapeDtypeStruct((t_rows, D_MODEL), F32),
                   jax.ShapeDtypeStruct((t_rows, D_MODEL), BF16)],
        compiler_params=_cparams(("parallel",)),
        name="out_proj",
    )(x2d, og, orr, ow, mod_l, w_out_b, gpost, gpre2)


def _ffn_kernel(h_ref, x1_ref, mod_ref, wa_ref, wg_ref, wd_ref, gpost_ref, o_ref, acc_ref, *, nf):
    j = pl.program_id(1)
    h = h_ref[...]
    act = (_silu(_dot(h, wa_ref[...])) * _dot(h, wg_ref[...])).astype(BF16)
    part = _dot(act, wd_ref[...])

    @pl.when(j == 0)
    def _():
        acc_ref[...] = part

    @pl.when(j > 0)
    def _():
        acc_ref[...] += part

    @pl.when(j == nf - 1)
    def _():
        gt2 = mod_ref[0][:, 5 * D_MODEL:6 * D_MODEL]
        o_ref[...] = x1_ref[...] + gt2 * (_row_rms(acc_ref[...]) * gpost_ref[...])


def _ffn(h2, x1, mod_l, w_gu_b, w_down_b, gpost, *, ctx, rows_per_batch):
    t_rows = h2.shape[0]
    tm, tf = TM_FFN, TF_FFN
    nf = D_FF // tf
    row = lambda i, j: (i, 0)
    return pl.pallas_call(
        functools.partial(_ffn_kernel, nf=nf),
        grid=(t_rows // tm, nf),
        in_specs=[pl.BlockSpec((tm, D_MODEL), row),
                  pl.BlockSpec((tm, D_MODEL), row),
                  pl.BlockSpec((1, 1, 6 * D_MODEL), _mod_index(ctx, rows_per_batch // tm)),
                  pl.BlockSpec((D_MODEL, tf), lambda i, j: (0, j)),
                  pl.BlockSpec((D_MODEL, tf), lambda i, j: (0, j + nf)),
                  pl.BlockSpec((tf, D_MODEL), lambda i, j: (j, 0)),
                  pl.BlockSpec((1, D_MODEL), lambda i, j: (0, 0))],
        out_specs=pl.BlockSpec((tm, D_MODEL), row),
        out_shape=jax.ShapeDtypeStruct((t_rows, D_MODEL), F32),
        scratch_shapes=[pltpu.VMEM((tm, D_MODEL), F32)],
        compiler_params=_cparams(("parallel", "arbitrary")),
        name="ffn",
    )(h2, x1, mod_l, w_gu_b, w_gu_b, w_down_b, gpost)


def _rope_tables(n):
    half = HEAD_DIM // 4
    pos = jnp.arange(n, dtype=jnp.int32)
    row = (pos // GRID_W).astype(F32)
    col = (pos % GRID_W).astype(F32)
    freqs = ROPE_BASE ** (-jnp.arange(half, dtype=F32) / half)
    ang_r = row[:, None] * freqs[None, :]
    ang_c = col[:, None] * freqs[None, :]
    ang = jnp.concatenate([ang_r, ang_r, ang_c, ang_c], axis=-1)
    second = (jnp.arange(HEAD_DIM) % (2 * half)) >= half
    cos = jnp.cos(ang)
    sin = jnp.sin(ang)
    sin_a = jnp.where(second[None, :], sin, 0.0)
    sin_b = jnp.where(second[None, :], 0.0, -sin)
    tile = lambda t: jnp.concatenate([t, t], axis=-1)
    return tile(cos), tile(sin_a), tile(sin_b)


def _lane_rows(v):
    t = jnp.zeros((8,), F32).at[:v.shape[0]].set(v.astype(F32))
    return jnp.broadcast_to(t[:, None], (8, LANES))


def _state_to_blockdiag(s):
    b = s.shape[0]
    s = s.reshape(b, RET_HEADS // 2, 2, HEAD_DIM, HEAD_DIM)
    eye = jnp.eye(2, dtype=s.dtype)
    bd = s[:, :, :, :, None, :] * eye[None, None, :, None, :, None]
    return bd.reshape(b, RET_HEADS // 2, LANES, LANES)


def _blockdiag_to_state(s):
    b = s.shape[0]
    s = s.reshape(b, RET_HEADS // 2, 2, HEAD_DIM, 2, HEAD_DIM)
    d = jnp.stack([s[:, :, 0, :, 0, :], s[:, :, 1, :, 1, :]], axis=2)
    return d.reshape(b, RET_HEADS, HEAD_DIM, HEAD_DIM)


def kernel(x_prompt, x_sample, cache_glob_k, cache_glob_v, state_ret_fwd, state_ret_bwd, cache_win_k, cache_win_v, c, c_ctx, w_mod, b_mod, g_pre_mix, g_post_mix, g_pre_ffn, g_post_ffn, w_in, g_q, g_k, ret_decay_fwd, ret_decay_bwd, win_sink, w_out, w_gate_up, w_down):
    bc, nc_, _ = x_prompt.shape
    bs, ns, _ = x_sample.shape
    past = cache_glob_k.shape[2]

    cond8 = jnp.concatenate([c, c_ctx[None, :], jnp.zeros((MOD_ROWS - bs - 1, D_MODEL), F32)], axis=0)
    mod = _modulation(cond8, w_mod, b_mod)

    w_in_b = w_in.astype(BF16)
    w_out_b = w_out.astype(BF16)
    w_gu_b = w_gate_up.astype(BF16)
    w_down_b = w_down.astype(BF16)
    rope_tabs = _rope_tables(ns)
    seg = jnp.arange(LANES) // HEAD_DIM
    bd = jnp.where(seg[:, None] == seg[None, :], 1.0 / HEAD_DIM, 0.0).astype(BF16)

    y = x_prompt.reshape(bc * nc_, D_MODEL)
    z = x_sample.reshape(bs * ns, D_MODEL)
    new = [[] for _ in range(6)]
    for l in range(DEPTH):
        mod_l = mod[l].reshape(MOD_ROWS, 1, 6 * D_MODEL)
        gq_t = jnp.tile(g_q[l], 2)[None, :]
        gk_t = jnp.tile(g_k[l], 2)[None, :]
        dec_tab = _lane_rows(jnp.concatenate([ret_decay_fwd[l], ret_decay_bwd[l]]))
        sink_tab = _lane_rows(win_sink[l])
        gpre, gpost = g_pre_mix[l][None, :], g_post_mix[l][None, :]
        gpre2, gpost2 = g_pre_ffn[l][None, :], g_post_ffn[l][None, :]

        qg, kg, vg, r, qw, kw, vw, kg32, vg32, kw32, vw32 = _in_proj(
            y, mod_l, gpre, w_in_b[l], gq_t, gk_t, bd, None, ctx=True, rows_per_batch=nc_)
        b3 = lambda t: t.reshape(bc, nc_, t.shape[-1])
        og = _attention(b3(qg), [(b3(kg), b3(vg))], ("full",), None, tq=nc_, name="attn_glob_ctx")
        ow = _attention(b3(qw), [(b3(kw), b3(vw))], ("full",), sink_tab, tq=nc_, name="attn_win_ctx")
        orr, sf, sb = _retention(b3(r), dec_tab, bd, None, emit_state=True, name="ret_ctx")
        flat = lambda t: t.reshape(bc * nc_, t.shape[-1])
        x1, h2 = _out_proj(y, flat(og), flat(orr), flat(ow), mod_l, w_out_b[l], gpost, gpre2,
                           ctx=True, rows_per_batch=nc_)
        y = _ffn(h2, x1, mod_l, w_gu_b[l], w_down_b[l], gpost2, ctx=True, rows_per_batch=nc_)
        kvshape = (bc, nc_, KV_HEADS, HEAD_DIM)
        new[0].append(kg32.reshape(kvshape))
        new[1].append(vg32.reshape(kvshape))
        new[2].append(_blockdiag_to_state(sf))
        new[3].append(_blockdiag_to_state(sb))
        new[4].append(kw32.reshape(kvshape))
        new[5].append(vw32.reshape(kvshape))

        qg, kg, vg, r, qw, kw, vw = _in_proj(
            z, mod_l, gpre, w_in_b[l], gq_t, gk_t, bd, rope_tabs, ctx=False, rows_per_batch=ns)
        s3 = lambda t: t.reshape(bs, ns, t.shape[-1])
        cache = lambda t: t[:, l].reshape(bs, past, LANES)
        og = _attention(s3(qg), [(s3(kg), s3(vg)), (cache(cache_glob_k), cache(cache_glob_v))],
                        ("full", "full"), None, tq=256, name="attn_glob_smp")
        ow = _attention(s3(qw), [(s3(kw), s3(vw)), (cache(cache_win_k), cache(cache_win_v))],
                        ("band", "full"), sink_tab, tq=128, name="attn_win_smp")
        init = (_state_to_blockdiag(state_ret_fwd[:, l]), _state_to_blockdiag(state_ret_bwd[:, l]))
        (orr,) = _retention(s3(r), dec_tab, bd, init, emit_state=False, name="ret_smp")
        flat = lambda t: t.reshape(bs * ns, t.shape[-1])
        x1, h2 = _out_proj(z, flat(og), flat(orr), flat(ow), mod_l, w_out_b[l], gpost, gpre2,
                           ctx=False, rows_per_batch=ns)
        z = _ffn(h2, x1, mod_l, w_gu_b[l], w_down_b[l], gpost2, ctx=False, rows_per_batch=ns)

    outs = [jnp.stack(t, axis=1) for t in new]
    return (y.reshape(bc, nc_, D_MODEL), z.reshape(bs, ns, D_MODEL),
            outs[0], outs[1], outs[2], outs[3], outs[4], outs[5])
```

```python
import functools

import jax
import jax.numpy as jnp
from jax import lax
from jax.experimental import pallas as pl
from jax.experimental.pallas import tpu as pltpu

F32 = jnp.float32
BF16 = jnp.bfloat16

D_MODEL = 1024
DEPTH = 4
GRID_W = 64
HEAD_DIM = 64
LANES = 128
GLOB_HEADS = 6
RET_HEADS = 4
WIN_HEADS = 6
KV_HEADS = 2
GROUP = GLOB_HEADS // KV_HEADS
WINDOW = 128
ROPE_BASE = 10000.0
QK_SCALE = HEAD_DIM ** -0.5
D_FF = 2816
IN_WIDTH = 2304
EPS = 1e-6
NEG = -1e30
MOD_ROWS = 8
CTX_ROW = 4

GQ0, GK0, GV0 = 0, 384, 512
RET0 = 640
WQ0, WK0, WV0 = 1664, 2048, 2176

VMEM_LIMIT = 56 * 1024 * 1024

TM_IN = 512
TM_POST = 512
FF_CHUNKS = ((0, 1024), (1024, 2048), (2048, D_FF))
RET_CHUNK = 256
TQ = 256


def _cparams(sem):
    return pltpu.CompilerParams(dimension_semantics=sem, vmem_limit_bytes=VMEM_LIMIT)


def _resident(block_shape, index_map):
    return pl.BlockSpec(block_shape, index_map, pipeline_mode=pl.Buffered(1))


def _dot(a, b):
    return jnp.dot(a, b, preferred_element_type=F32)


def _dot_nt(a, b):
    return lax.dot_general(a, b, (((1,), (1,)), ((), ())), preferred_element_type=F32)


def _dot_tn(a, b):
    return lax.dot_general(a, b, (((0,), (0,)), ((), ())), preferred_element_type=F32)


def _silu(x):
    return x / (1.0 + jnp.exp(-x))


def _row_rms(x):
    return x * lax.rsqrt(jnp.mean(x * x, axis=-1, keepdims=True) + EPS)


def _mod_kernel(c_ref, w_ref, b_ref, o_ref):
    s = _silu(c_ref[...]).astype(BF16)
    o_ref[0] = _dot(s, w_ref[0].astype(BF16)) + b_ref[0]


def _modulation(cond8, w_mod, b_mod):
    tn = 1536
    n_out = 6 * D_MODEL
    return pl.pallas_call(
        _mod_kernel,
        grid=(DEPTH, n_out // tn),
        in_specs=[pl.BlockSpec((MOD_ROWS, D_MODEL), lambda l, j: (0, 0)),
                  pl.BlockSpec((1, D_MODEL, tn), lambda l, j: (l, 0, j)),
                  pl.BlockSpec((1, 1, tn), lambda l, j: (l, 0, j))],
        out_specs=pl.BlockSpec((1, MOD_ROWS, tn), lambda l, j: (l, 0, j)),
        out_shape=jax.ShapeDtypeStruct((DEPTH, MOD_ROWS, n_out), F32),
        compiler_params=_cparams(("parallel", "parallel")),
        name="modulation",
    )(cond8, w_mod, b_mod.reshape(DEPTH, 1, n_out))


def _mod_spec(layer, ctx, tiles_per_batch):
    if ctx:
        return pl.BlockSpec((1, 1, 6 * D_MODEL), lambda i: (layer * MOD_ROWS + CTX_ROW, 0, 0))
    return pl.BlockSpec((1, 1, 6 * D_MODEL), lambda i: (layer * MOD_ROWS + i // tiles_per_batch, 0, 0))


def _in_proj_kernel(*refs, rope, ctx):
    x_ref, mod_ref, gpre_ref, w_ref, gq_ref, gk_ref, bd_ref = refs[:7]
    pos = 7
    if rope:
        cos_ref, sa_ref, sb_ref = refs[pos:pos + 3]
        pos += 3
    qg_ref, kg_ref, vgt_ref, r_ref, qw_ref, kw_ref, vwt_ref = refs[pos:pos + 7]
    pos += 7
    if ctx:
        kg32_ref, vg32_ref, kw32_ref, vw32_ref = refs[pos:pos + 4]

    x = x_ref[...]
    m = mod_ref[0]
    sh1 = m[:, 0:D_MODEL]
    sc1 = m[:, D_MODEL:2 * D_MODEL]
    h = _row_rms(x) * (gpre_ref[...] * (1.0 + sc1)) + sh1
    hb = h.astype(BF16)
    tm = x.shape[0]
    lane_lo = lax.broadcasted_iota(jnp.int32, (tm, LANES), 1) < HEAD_DIM

    def head_norm(t, g):
        ss = _dot((t * t).astype(BF16), bd_ref[...])
        return t * lax.rsqrt(ss + EPS) * g

    def rot(t):
        if not rope:
            return t
        return (t * cos_ref[...] + pltpu.roll(t, 16, 1) * sa_ref[...]
                + pltpu.roll(t, LANES - 16, 1) * sb_ref[...])

    def store_q(slabs, q_ref):
        for hd in range(GLOB_HEADS):
            t = slabs[hd // 2]
            kv = hd // GROUP
            if hd % 2 != kv:
                t = pltpu.roll(t, HEAD_DIM, 1)
            keep = lane_lo if kv == 0 else jnp.logical_not(lane_lo)
            q_ref[:, hd * LANES:(hd + 1) * LANES] = jnp.where(keep, t * QK_SCALE, 0.0).astype(BF16)

    pg = _dot(hb, w_ref[:, GQ0:RET0])
    store_q([rot(head_norm(pg[:, s * LANES:(s + 1) * LANES], gq_ref[...])) for s in range(3)], qg_ref)
    kg = head_norm(pg[:, GK0:GK0 + LANES], gk_ref[...])
    vg = pg[:, GV0:GV0 + LANES]
    if ctx:
        kg32_ref[...] = kg
        vg32_ref[...] = vg
    kg_ref[...] = rot(kg).astype(BF16)
    vgt_ref[...] = vg.T.astype(BF16)

    pr = _dot(hb, w_ref[:, RET0:WQ0])
    r_ref[:, 0:256] = pr[:, 0:256].astype(BF16)
    r_ref[:, 256:512] = (pr[:, 256:512] * QK_SCALE).astype(BF16)
    r_ref[:, 512:1024] = pr[:, 512:1024].astype(BF16)

    pw = _dot(hb, w_ref[:, WQ0:IN_WIDTH])
    store_q([rot(pw[:, s * LANES:(s + 1) * LANES]) for s in range(3)], qw_ref)
    kw = pw[:, WK0 - WQ0:WK0 - WQ0 + LANES]
    vw = pw[:, WV0 - WQ0:WV0 - WQ0 + LANES]
    if ctx:
        kw32_ref[...] = kw
        vw32_ref[...] = vw
    kw_ref[...] = rot(kw).astype(BF16)
    vwt_ref[...] = vw.T.astype(BF16)


def _in_proj(x2d, mod, gpre, w_in_b, gq_t, gk_t, bd, rope_tabs, *, layer, ctx, rows_per_batch):
    t_rows = x2d.shape[0]
    tm = TM_IN
    tiles_per_batch = rows_per_batch // tm
    rope = rope_tabs is not None
    const2 = lambda i: (0, 0)
    lay3 = lambda i: (layer, 0, 0)
    row = lambda i: (i, 0)
    col = lambda i: (0, i)
    in_specs = [pl.BlockSpec((tm, D_MODEL), row),
                _mod_spec(layer, ctx, tiles_per_batch),
                pl.BlockSpec((None, 1, D_MODEL), lay3),
                _resident((None, D_MODEL, IN_WIDTH), lay3),
                pl.BlockSpec((None, 1, LANES), lay3),
                pl.BlockSpec((None, 1, LANES), lay3),
                pl.BlockSpec((LANES, LANES), const2)]
    args = [x2d, mod, gpre, w_in_b, gq_t, gk_t, bd]
    if rope:
        tab = pl.BlockSpec((tm, LANES), lambda i: (i % tiles_per_batch, 0))
        in_specs += [tab, tab, tab]
        args += list(rope_tabs)
    bf = lambda shape: jax.ShapeDtypeStruct(shape, BF16)
    out_shape = [bf((t_rows, GLOB_HEADS * LANES)), bf((t_rows, LANES)), bf((LANES, t_rows)),
                 bf((t_rows, 4 * RET_HEADS * HEAD_DIM)),
                 bf((t_rows, WIN_HEADS * LANES)), bf((t_rows, LANES)), bf((LANES, t_rows))]
    out_specs = [pl.BlockSpec((tm, GLOB_HEADS * LANES), row), pl.BlockSpec((tm, LANES), row),
                 pl.BlockSpec((LANES, tm), col),
                 pl.BlockSpec((tm, 4 * RET_HEADS * HEAD_DIM), row),
                 pl.BlockSpec((tm, WIN_HEADS * LANES), row), pl.BlockSpec((tm, LANES), row),
                 pl.BlockSpec((LANES, tm), col)]
    if ctx:
        out_shape += [jax.ShapeDtypeStruct((t_rows, LANES), F32)] * 4
        out_specs += [pl.BlockSpec((tm, LANES), row)] * 4
    return pl.pallas_call(
        functools.partial(_in_proj_kernel, rope=rope, ctx=ctx),
        grid=(t_rows // tm,),
        in_specs=in_specs, out_specs=out_specs, out_shape=out_shape,
        compiler_params=_cparams(("parallel",)),
        name="in_proj_ctx" if ctx else "in_proj_smp",
    )(*args)


def _attn_kernel(*refs, kinds, sink_layer, tq, n_lat):
    q_ref = refs[0]
    pos = 1
    srcs = []
    for kind in kinds:
        cnt = 2 if kind == "full" else 6
        srcs.append((kind, refs[pos:pos + cnt]))
        pos += cnt
    if sink_layer is not None:
        sink_ref = refs[pos]
        pos += 1
    o_ref = refs[pos]
    j = pl.program_id(1)
    cols = GROUP * tq
    col_q = lax.broadcasted_iota(jnp.int32, (1, cols), 1)

    pieces = []
    for kv in range(KV_HEADS):
        half = slice(HEAD_DIM * kv, HEAD_DIM * (kv + 1))
        qs = jnp.concatenate(
            [q_ref[:, (GROUP * kv + g) * LANES:(GROUP * kv + g + 1) * LANES] for g in range(GROUP)], axis=0)
        parts = []
        for kind, r in srcs:
            if kind == "full":
                st = _dot_nt(r[0][...], qs)
                vt = r[1][half, :]
            else:
                width = tq + 2 * WINDOW
                k = jnp.concatenate([r[0][...], r[1][...], r[2][...]], axis=0)
                vt = jnp.concatenate([r[3][half, :], r[4][half, :], r[5][half, :]], axis=1)
                st = _dot_nt(k, qs)
                krel = lax.broadcasted_iota(jnp.int32, (width, cols), 0) - WINDOW
                qrel = lax.broadcasted_iota(jnp.int32, (width, cols), 1) & (tq - 1)
                kpos = krel + j * tq
                ok = (jnp.abs(krel - qrel) <= WINDOW) & (kpos >= 0) & (kpos < n_lat)
                st = jnp.where(ok, st, NEG)
            parts.append((st, vt))
        m = parts[0][0].max(axis=0, keepdims=True)
        for st, _ in parts[1:]:
            m = jnp.maximum(m, st.max(axis=0, keepdims=True))
        if sink_layer is not None:
            s0 = sink_ref[sink_layer, GROUP * kv]
            s1 = sink_ref[sink_layer, GROUP * kv + 1]
            s2 = sink_ref[sink_layer, GROUP * kv + 2]
            sk = jnp.where(col_q < tq, s0, jnp.where(col_q < 2 * tq, s1, s2))
            m = jnp.maximum(m, sk)
            den = jnp.exp(sk - m)
        else:
            den = jnp.zeros_like(m)
        acc = jnp.zeros((HEAD_DIM, cols), F32)
        for st, vt in parts:
            p = jnp.exp(st - m)
            den = den + p.sum(axis=0, keepdims=True)
            acc = acc + _dot(vt, p.astype(BF16))
        o = acc / den
        for g in range(GROUP):
            pieces.append(o[:, g * tq:(g + 1) * tq])

    for slab in range(GLOB_HEADS // 2):
        t = jnp.concatenate([pieces[2 * slab], pieces[2 * slab + 1]], axis=0)
        o_ref[:, slab * LANES:(slab + 1) * LANES] = t.T.astype(BF16)


def _attention(q, sources, kinds, sink, sink_layer, *, bsz, n, name):
    tq = TQ
    nq = n // tq
    in_specs = [pl.BlockSpec((tq, q.shape[1]), lambda b, j: (b * nq + j, 0))]
    args = [q]
    for arr, spec in sources:
        in_specs.append(spec)
        args.append(arr)
    if sink is not None:
        in_specs.append(pl.BlockSpec(memory_space=pltpu.SMEM))
        args.append(sink)
    ow = GLOB_HEADS * HEAD_DIM
    return pl.pallas_call(
        functools.partial(_attn_kernel, kinds=kinds, sink_layer=sink_layer if sink is not None else None,
                          tq=tq, n_lat=n),
        grid=(bsz, nq),
        in_specs=in_specs,
        out_specs=pl.BlockSpec((tq, ow), lambda b, j: (b * nq + j, 0)),
        out_shape=jax.ShapeDtypeStruct((bsz * n, ow), BF16),
        compiler_params=_cparams(("parallel", "parallel")),
        name=name,
    )(*args)


def _full_source(k, vt, n):
    return [(k, pl.BlockSpec((n, LANES), lambda b, j: (b, 0))),
            (vt, pl.BlockSpec((LANES, n), lambda b, j: (0, b)))]


def _cache_source(k_all, vt_all, layer, past):
    return [(k_all, pl.BlockSpec((None, past, LANES), lambda b, j: (layer, b, 0))),
            (vt_all, pl.BlockSpec((None, LANES, past), lambda b, j: (layer, 0, b)))]


def _band_source(k, vt, n):
    tq = TQ
    per = n // WINDOW
    step = tq // WINDOW
    prev = lambda b, j: b * per + jnp.maximum(step * j - 1, 0)
    nxt = lambda b, j: b * per + jnp.minimum(step * j + step, per - 1)
    mid = lambda b, j: b * (n // tq) + j
    return [(k, pl.BlockSpec((WINDOW, LANES), lambda b, j: (prev(b, j), 0))),
            (k, pl.BlockSpec((tq, LANES), lambda b, j: (mid(b, j), 0))),
            (k, pl.BlockSpec((WINDOW, LANES), lambda b, j: (nxt(b, j), 0))),
            (vt, pl.BlockSpec((LANES, WINDOW), lambda b, j: (0, prev(b, j)))),
            (vt, pl.BlockSpec((LANES, tq), lambda b, j: (0, mid(b, j)))),
            (vt, pl.BlockSpec((LANES, WINDOW), lambda b, j: (0, nxt(b, j))))]


def _ret_kernel(*refs, n, chunk, has_init, emit_state):
    r_ref, dec_ref, bd_ref = refs[:3]
    pos = 3
    if has_init:
        s0f_ref, s0b_ref = refs[pos:pos + 2]
        pos += 2
    o_ref = refs[pos]
    pos += 1
    if emit_state:
        sf_ref, sb_ref = refs[pos:pos + 2]
        pos += 2
    stf, stb, dsum, dec, cdec = refs[pos:pos + 5]
    nc = n // chunk
    n_slab = RET_HEADS // 2
    use_inter = has_init or nc > 1
    lane_lo = lax.broadcasted_iota(jnp.int32, (chunk, LANES), 1) < HEAD_DIM
    row_s = lax.broadcasted_iota(jnp.int32, (LANES, LANES), 0) < HEAD_DIM
    lane_s = lax.broadcasted_iota(jnp.int32, (LANES, LANES), 1) < HEAD_DIM
    blockdiag = row_s == lane_s

    @pl.when(pl.program_id(0) == 0)
    def _():
        x = dec_ref[...]
        lg = jnp.minimum(x, 0.0) - jnp.log(1.0 + jnp.exp(-jnp.abs(x)))
        rowc = lax.broadcasted_iota(jnp.int32, (chunk, LANES), 0).astype(F32)
        diff = (lax.broadcasted_iota(jnp.int32, (chunk, chunk), 0)
                - lax.broadcasted_iota(jnp.int32, (chunk, chunk), 1)).astype(F32)
        for hd in range(RET_HEADS):
            lf = lg[hd:hd + 1, 0:1]
            lb = lg[RET_HEADS + hd:RET_HEADS + hd + 1, 0:1]
            dsum[hd] = (jnp.where(diff >= 0, jnp.exp(jnp.maximum(diff, 0.0) * lf), 0.0)
                        + jnp.where(diff <= 0, jnp.exp(jnp.maximum(-diff, 0.0) * lb), 0.0))
        for sl in range(n_slab):
            f0, f1 = lg[2 * sl:2 * sl + 1], lg[2 * sl + 1:2 * sl + 2]
            b0 = lg[RET_HEADS + 2 * sl:RET_HEADS + 2 * sl + 1]
            b1 = lg[RET_HEADS + 2 * sl + 1:RET_HEADS + 2 * sl + 2]
            lf2 = jnp.where(lane_lo, f0, f1)
            lb2 = jnp.where(lane_lo, b0, b1)
            dec[0 + sl] = jnp.exp((chunk - 1.0 - rowc) * lf2)
            dec[2 + sl] = jnp.exp(rowc * lb2)
            dec[4 + sl] = jnp.exp((rowc + 1.0) * lf2)
            dec[6 + sl] = jnp.exp((chunk - rowc) * lb2)
            cdec[sl] = jnp.exp(chunk * jnp.where(row_s, f0, f1))
            cdec[n_slab + sl] = jnp.exp(chunk * jnp.where(row_s, b0, b1))

    for sl in range(n_slab):
        stf[0, sl] = s0f_ref[0, sl] if has_init else jnp.zeros((LANES, LANES), F32)
        stb[nc, sl] = s0b_ref[0, sl] if has_init else jnp.zeros((LANES, LANES), F32)

    def local_state(ci, carry):
        base = pl.multiple_of(ci * chunk, chunk)
        for sl in range(n_slab):
            k2 = r_ref[pl.ds(base, chunk), 256 + sl * LANES:256 + (sl + 1) * LANES].astype(F32)
            v2 = r_ref[pl.ds(base, chunk), 512 + sl * LANES:512 + (sl + 1) * LANES]
            kk = jnp.concatenate([(k2 * dec[0 + sl]).astype(BF16), (k2 * dec[2 + sl]).astype(BF16)], axis=1)
            u = _dot_tn(kk, v2)
            stf[ci + 1, sl] = jnp.where(blockdiag, u[:LANES], 0.0)
            stb[ci, sl] = jnp.where(blockdiag, u[LANES:], 0.0)
        return carry

    lax.fori_loop(0, nc, local_state, 0)
    for sl in range(n_slab):
        for ci in range(nc):
            stf[ci + 1, sl] = stf[ci, sl] * cdec[sl] + stf[ci + 1, sl]
        for ci in range(nc - 1, -1, -1):
            stb[ci, sl] = stb[ci + 1, sl] * cdec[n_slab + sl] + stb[ci, sl]
        if emit_state:
            sf_ref[0, sl] = stf[nc, sl]
            sb_ref[0, sl] = stb[0, sl]

    def outputs(ci, carry):
        base = pl.multiple_of(ci * chunk, chunk)
        for sl in range(n_slab):
            cols = lambda off: slice(off + sl * LANES, off + (sl + 1) * LANES)
            q2 = r_ref[pl.ds(base, chunk), cols(0)]
            k2 = r_ref[pl.ds(base, chunk), cols(256)]
            v2 = r_ref[pl.ds(base, chunk), cols(512)]
            g2 = r_ref[pl.ds(base, chunk), cols(768)].astype(F32)
            zero = jnp.zeros_like(q2)
            a_lo = (_dot_nt(jnp.where(lane_lo, q2, zero), k2) * dsum[2 * sl]).astype(BF16)
            a_hi = (_dot_nt(jnp.where(lane_lo, zero, q2), k2) * dsum[2 * sl + 1]).astype(BF16)
            o2 = jnp.where(lane_lo, _dot(a_lo, v2), _dot(a_hi, v2))
            if use_inter:
                st = jnp.concatenate([stf[ci, sl], stb[ci + 1, sl]], axis=1).astype(BF16)
                it = _dot(q2, st)
                o2 = o2 + it[:, :LANES] * dec[4 + sl] + it[:, LANES:] * dec[6 + sl]
            ss = _dot((o2 * o2).astype(BF16), bd_ref[...])
            y = o2 * lax.rsqrt(ss + EPS) * _silu(g2)
            o_ref[pl.ds(base, chunk), sl * LANES:(sl + 1) * LANES] = y.astype(BF16)
        return carry

    lax.fori_loop(0, nc, outputs, 0)


def _retention(r, dec_tab, bd, init, *, layer, bsz, n, emit_state, name):
    rw = r.shape[1]
    chunk = RET_CHUNK
    nc = n // chunk
    n_slab = RET_HEADS // 2
    has_init = init is not None
    in_specs = [pl.BlockSpec((n, rw), lambda b: (b, 0)),
                pl.BlockSpec((None, 8, LANES), lambda b: (layer, 0, 0)),
                pl.BlockSpec((LANES, LANES), lambda b: (0, 0))]
    args = [r, dec_tab, bd]
    if has_init:
        init_spec = pl.BlockSpec((1, None, n_slab, LANES, LANES), lambda b: (b, layer, 0, 0, 0))
        in_specs += [init_spec, init_spec]
        args += list(init)
    ow = RET_HEADS * HEAD_DIM
    out_shape = [jax.ShapeDtypeStruct((bsz * n, ow), BF16)]
    out_specs = [pl.BlockSpec((n, ow), lambda b: (b, 0))]
    if emit_state:
        st_spec = pl.BlockSpec((1, n_slab, LANES, LANES), lambda b: (b, 0, 0, 0))
        out_shape += [jax.ShapeDtypeStruct((bsz, n_slab, LANES, LANES), F32)] * 2
        out_specs += [st_spec, st_spec]
    return pl.pallas_call(
        functools.partial(_ret_kernel, n=n, chunk=chunk, has_init=has_init, emit_state=emit_state),
        grid=(bsz,),
        in_specs=in_specs, out_specs=out_specs, out_shape=out_shape,
        scratch_shapes=[pltpu.VMEM((nc + 1, n_slab, LANES, LANES), F32),
                        pltpu.VMEM((nc + 1, n_slab, LANES, LANES), F32),
                        pltpu.VMEM((RET_HEADS, chunk, chunk), F32),
                        pltpu.VMEM((8, chunk, LANES), F32),
                        pltpu.VMEM((2 * n_slab, LANES, LANES), F32)],
        compiler_params=_cparams(("arbitrary",)),
        name=name,
    )(*args)


def _post_kernel(x_ref, og_ref, or_ref, ow_ref, mod_ref, wo_ref, wgu_ref, wd_ref,
                 gpost_ref, gpre2_ref, gpost2_ref, o_ref):
    m = mod_ref[0]
    gt1 = m[:, 2 * D_MODEL:3 * D_MODEL]
    sh2 = m[:, 3 * D_MODEL:4 * D_MODEL]
    sc2 = m[:, 4 * D_MODEL:5 * D_MODEL]
    gt2 = m[:, 5 * D_MODEL:6 * D_MODEL]
    merged = jnp.concatenate([og_ref[...], or_ref[...], ow_ref[...]], axis=1)
    mixed = _dot(merged, wo_ref[...])
    x1 = x_ref[...] + gt1 * (_row_rms(mixed) * gpost_ref[...])
    h2 = (_row_rms(x1) * (gpre2_ref[...] * (1.0 + sc2)) + sh2).astype(BF16)
    ff = None
    for lo, hi in FF_CHUNKS:
        act = (_silu(_dot(h2, wgu_ref[:, lo:hi])) * _dot(h2, wgu_ref[:, D_FF + lo:D_FF + hi])).astype(BF16)
        part = _dot(act, wd_ref[lo:hi, :])
        ff = part if ff is None else ff + part
    o_ref[...] = x1 + gt2 * (_row_rms(ff) * gpost2_ref[...])


def _post(x2d, og, orr, ow, mod, w_out_b, w_gu_b, w_down_b, gpost, gpre2, gpost2, *, layer, ctx, rows_per_batch):
    t_rows = x2d.shape[0]
    tm = TM_POST
    row = lambda i: (i, 0)
    lay3 = lambda i: (layer, 0, 0)
    gain = pl.BlockSpec((None, 1, D_MODEL), lay3)
    return pl.pallas_call(
        _post_kernel,
        grid=(t_rows // tm,),
        in_specs=[pl.BlockSpec((tm, D_MODEL), row),
                  pl.BlockSpec((tm, og.shape[1]), row),
                  pl.BlockSpec((tm, orr.shape[1]), row),
                  pl.BlockSpec((tm, ow.shape[1]), row),
                  _mod_spec(layer, ctx, rows_per_batch // tm),
                  _resident((None, D_MODEL, D_MODEL), lay3),
                  _resident((None, D_MODEL, 2 * D_FF), lay3),
                  _resident((None, D_FF, D_MODEL), lay3),
                  gain, gain, gain],
        out_specs=pl.BlockSpec((tm, D_MODEL), row),
        out_shape=jax.ShapeDtypeStruct((t_rows, D_MODEL), F32),
        compiler_params=_cparams(("parallel",)),
        name="post_mixer",
    )(x2d, og, orr, ow, mod, w_out_b, w_gu_b, w_down_b, gpost, gpre2, gpost2)


def _rope_tables(n):
    half = HEAD_DIM // 4
    pos = jnp.arange(n, dtype=jnp.int32)
    row = (pos // GRID_W).astype(F32)
    col = (pos % GRID_W).astype(F32)
    freqs = ROPE_BASE ** (-jnp.arange(half, dtype=F32) / half)
    ang_r = row[:, None] * freqs[None, :]
    ang_c = col[:, None] * freqs[None, :]
    ang = jnp.concatenate([ang_r, ang_r, ang_c, ang_c], axis=-1)
    second = (jnp.arange(HEAD_DIM) % (2 * half)) >= half
    cos = jnp.cos(ang)
    sin = jnp.sin(ang)
    sin_a = jnp.where(second[None, :], sin, 0.0)
    sin_b = jnp.where(second[None, :], 0.0, -sin)
    tile = lambda t: jnp.concatenate([t, t], axis=-1)
    return tile(cos), tile(sin_a), tile(sin_b)


def _state_to_blockdiag(s):
    b = s.shape[0]
    s = s.reshape(b, DEPTH, RET_HEADS // 2, 2, HEAD_DIM, HEAD_DIM)
    eye = jnp.eye(2, dtype=s.dtype)
    bd = s[:, :, :, :, :, None, :] * eye[None, None, None, :, None, :, None]
    return bd.reshape(b, DEPTH, RET_HEADS // 2, LANES, LANES)


def _blockdiag_to_state(s):
    b = s.shape[0]
    s = s.reshape(b, DEPTH, RET_HEADS // 2, 2, HEAD_DIM, 2, HEAD_DIM)
    d = jnp.stack([s[:, :, :, 0, :, 0, :], s[:, :, :, 1, :, 1, :]], axis=3)
    return d.reshape(b, DEPTH, RET_HEADS, HEAD_DIM, HEAD_DIM)


def _cache_layouts(ck, cv):
    b, _, past = ck.shape[:3]
    k = jnp.transpose(ck.reshape(b, DEPTH, past, LANES), (1, 0, 2, 3)).reshape(DEPTH, b * past, LANES)
    vt = jnp.transpose(cv.reshape(b, DEPTH, past, LANES), (1, 3, 0, 2)).reshape(DEPTH, LANES, b * past)
    return k.astype(BF16), vt.astype(BF16)


def kernel(x_prompt, x_sample, cache_glob_k, cache_glob_v, state_ret_fwd, state_ret_bwd, cache_win_k, cache_win_v, c, c_ctx, w_mod, b_mod, g_pre_mix, g_post_mix, g_pre_ffn, g_post_ffn, w_in, g_q, g_k, ret_decay_fwd, ret_decay_bwd, win_sink, w_out, w_gate_up, w_down):
    bc, nc_, _ = x_prompt.shape
    bs, ns, _ = x_sample.shape
    past = cache_glob_k.shape[2]

    cond8 = jnp.concatenate([c, c_ctx[None, :], jnp.zeros((MOD_ROWS - bs - 1, D_MODEL), F32)], axis=0)
    mod = _modulation(cond8, w_mod, b_mod).reshape(DEPTH * MOD_ROWS, 1, 6 * D_MODEL)

    w_in_b = w_in.astype(BF16)
    w_out_b = w_out.astype(BF16)
    w_gu_b = w_gate_up.astype(BF16)
    w_down_b = w_down.astype(BF16)
    rope_tabs = _rope_tables(ns)
    seg = jnp.arange(LANES) // HEAD_DIM
    bd = jnp.where(seg[:, None] == seg[None, :], 1.0 / HEAD_DIM, 0.0).astype(BF16)
    gain3 = lambda g: g.reshape(DEPTH, 1, D_MODEL)
    gpre, gpost, gpre2, gpost2 = gain3(g_pre_mix), gain3(g_post_mix), gain3(g_pre_ffn), gain3(g_post_ffn)
    gq_t = jnp.tile(g_q, (1, 2)).reshape(DEPTH, 1, LANES)
    gk_t = jnp.tile(g_k, (1, 2)).reshape(DEPTH, 1, LANES)
    dec_tab = jnp.broadcast_to(jnp.concatenate([ret_decay_fwd, ret_decay_bwd], axis=1)[:, :, None],
                               (DEPTH, 2 * RET_HEADS, LANES))
    init = (_state_to_blockdiag(state_ret_fwd), _state_to_blockdiag(state_ret_bwd))
    cgk, cgvt = _cache_layouts(cache_glob_k, cache_glob_v)
    cwk, cwvt = _cache_layouts(cache_win_k, cache_win_v)

    y = x_prompt.reshape(bc * nc_, D_MODEL)
    z = x_sample.reshape(bs * ns, D_MODEL)
    new = [[] for _ in range(6)]
    for l in range(DEPTH):
        qg, kg, vgt, r, qw, kw, vwt, kg32, vg32, kw32, vw32 = _in_proj(
            y, mod, gpre, w_in_b, gq_t, gk_t, bd, None, layer=l, ctx=True, rows_per_batch=nc_)
        og = _attention(qg, _full_source(kg, vgt, nc_), ("full",), None, l,
                        bsz=bc, n=nc_, name="attn_glob_ctx")
        ow = _attention(qw, _full_source(kw, vwt, nc_), ("full",), win_sink, l,
                        bsz=bc, n=nc_, name="attn_win_ctx")
        orr, sf, sb = _retention(r, dec_tab, bd, None, layer=l, bsz=bc, n=nc_, emit_state=True, name="ret_ctx")
        y = _post(y, og, orr, ow, mod, w_out_b, w_gu_b, w_down_b, gpost, gpre2, gpost2,
                  layer=l, ctx=True, rows_per_batch=nc_)
        for lst, t in zip(new, (kg32, vg32, sf, sb, kw32, vw32)):
            lst.append(t)

        qg, kg, vgt, r, qw, kw, vwt = _in_proj(
            z, mod, gpre, w_in_b, gq_t, gk_t, bd, rope_tabs, layer=l, ctx=False, rows_per_batch=ns)
        og = _attention(qg, _full_source(kg, vgt, ns) + _cache_source(cgk, cgvt, l, past),
                        ("full", "full"), None, l, bsz=bs, n=ns, name="attn_glob_smp")
        ow = _attention(qw, _band_source(kw, vwt, ns) + _cache_source(cwk, cwvt, l, past),
                        ("band", "full"), win_sink, l, bsz=bs, n=ns, name="attn_win_smp")
        (orr,) = _retention(r, dec_tab, bd, init, layer=l, bsz=bs, n=ns, emit_state=False, name="ret_smp")
        z = _post(z, og, orr, ow, mod, w_out_b, w_gu_b, w_down_b, gpost, gpre2, gpost2,
                  layer=l, ctx=False, rows_per_batch=ns)

    kv5 = lambda ts: jnp.stack([t.reshape(bc, nc_, LANES) for t in ts], axis=1).reshape(
        bc, DEPTH, nc_, KV_HEADS, HEAD_DIM)
    st5 = lambda ts: _blockdiag_to_state(jnp.stack(ts, axis=1))
    return (y.reshape(bc, nc_, D_MODEL), z.reshape(bs, ns, D_MODEL),
            kv5(new[0]), kv5(new[1]), st5(new[2]), st5(new[3]), kv5(new[4]), kv5(new[5]))
```

```python
import functools

import jax
import jax.numpy as jnp
from jax import lax
from jax.experimental import pallas as pl
from jax.experimental.pallas import tpu as pltpu

F32 = jnp.float32
BF16 = jnp.bfloat16

D_MODEL = 1024
DEPTH = 4
GRID_W = 64
HEAD_DIM = 64
LANES = 128
GLOB_HEADS = 6
RET_HEADS = 4
WIN_HEADS = 6
KV_HEADS = 2
GROUP = GLOB_HEADS // KV_HEADS
WINDOW = 128
ROPE_BASE = 10000.0
QK_SCALE = HEAD_DIM ** -0.5
LOG2E = 1.4426950408889634
D_FF = 2816
IN_WIDTH = 2304
EPS = 1e-6
NEG = -1e30
MOD_ROWS = 8
CTX_ROW = 4

GQ0, GK0, GV0 = 0, 384, 512
RET0 = 640
WQ0, WK0, WV0 = 1664, 2048, 2176

VMEM_LIMIT = 56 * 1024 * 1024

TM_IN = 512
IN_SUBTILES = 2
TM_POST = 512
POST_SUBTILES = 2
FF_CHUNKS = ((0, 1024), (1024, 2048), (2048, D_FF))
RET_CHUNK = 256
TQ = 256
ONES_ROWS = 16


def _cparams(sem):
    return pltpu.CompilerParams(dimension_semantics=sem, vmem_limit_bytes=VMEM_LIMIT)


def _resident(block_shape, index_map):
    return pl.BlockSpec(block_shape, index_map, pipeline_mode=pl.Buffered(1))


def _dot(a, b):
    return jnp.dot(a, b, preferred_element_type=F32)


def _dot_nt(a, b):
    return lax.dot_general(a, b, (((1,), (1,)), ((), ())), preferred_element_type=F32)


def _dot_tn(a, b):
    return lax.dot_general(a, b, (((0,), (0,)), ((), ())), preferred_element_type=F32)


def _silu(x):
    return x / (1.0 + jnp.exp(-x))


def _row_rms(x):
    return x * lax.rsqrt(jnp.mean(x * x, axis=-1, keepdims=True) + EPS)


def _mod_kernel(c_ref, w_ref, b_ref, o_ref):
    s = _silu(c_ref[...]).astype(BF16)
    o_ref[0] = _dot(s, w_ref[0].astype(BF16)) + b_ref[0]


def _modulation(cond8, w_mod, b_mod):
    tn = 1536
    n_out = 6 * D_MODEL
    return pl.pallas_call(
        _mod_kernel,
        grid=(DEPTH, n_out // tn),
        in_specs=[pl.BlockSpec((MOD_ROWS, D_MODEL), lambda l, j: (0, 0)),
                  pl.BlockSpec((1, D_MODEL, tn), lambda l, j: (l, 0, j)),
                  pl.BlockSpec((1, 1, tn), lambda l, j: (l, 0, j))],
        out_specs=pl.BlockSpec((1, MOD_ROWS, tn), lambda l, j: (l, 0, j)),
        out_shape=jax.ShapeDtypeStruct((DEPTH, MOD_ROWS, n_out), F32),
        compiler_params=_cparams(("parallel", "parallel")),
        name="modulation",
    )(cond8, w_mod, b_mod.reshape(DEPTH, 1, n_out))


def _mod_spec(layer, ctx, tiles_per_batch):
    if ctx:
        return pl.BlockSpec((1, 1, 6 * D_MODEL), lambda i: (layer * MOD_ROWS + CTX_ROW, 0, 0))
    return pl.BlockSpec((1, 1, 6 * D_MODEL), lambda i: (layer * MOD_ROWS + i // tiles_per_batch, 0, 0))


def _in_proj_kernel(*refs, rope, ctx):
    x_ref, mod_ref, gpre_ref, w_ref, gq_ref, gk_ref, bd_ref = refs[:7]
    pos = 7
    if rope:
        cos_ref, sa_ref, sb_ref = refs[pos:pos + 3]
        pos += 3
    qg_ref, kg_ref, vgt_ref, r_ref, qw_ref, kw_ref, vwt_ref = refs[pos:pos + 7]
    pos += 7
    if ctx:
        kg32_ref, vg32_ref, kw32_ref, vw32_ref = refs[pos:pos + 4]

    m = mod_ref[0]
    sh1 = m[:, 0:D_MODEL]
    sc1 = m[:, D_MODEL:2 * D_MODEL]
    sub = x_ref.shape[0] // IN_SUBTILES
    lane_lo = lax.broadcasted_iota(jnp.int32, (sub, LANES), 1) < HEAD_DIM

    def prologue(rows):
        return (_row_rms(x_ref[rows, :]) * (gpre_ref[...] * (1.0 + sc1)) + sh1).astype(BF16)

    def head_norm(t, g):
        ss = _dot((t * t).astype(BF16), bd_ref[...])
        return t * lax.rsqrt(ss + EPS) * g

    def rot(t, rows):
        if not rope:
            return t
        return (t * cos_ref[rows, :] + pltpu.roll(t, 16, 1) * sa_ref[rows, :]
                + pltpu.roll(t, LANES - 16, 1) * sb_ref[rows, :])

    def store_q(slabs, q_ref, rows):
        for hd in range(GLOB_HEADS):
            t = slabs[hd // 2]
            kv = hd // GROUP
            if hd % 2 != kv:
                t = pltpu.roll(t, HEAD_DIM, 1)
            keep = lane_lo if kv == 0 else jnp.logical_not(lane_lo)
            q_ref[rows, hd * LANES:(hd + 1) * LANES] = jnp.where(keep, t * (QK_SCALE * LOG2E), 0.0).astype(BF16)

    def glob_epilogue(pg, rows):
        store_q([rot(head_norm(pg[:, s * LANES:(s + 1) * LANES], gq_ref[...]), rows) for s in range(3)],
                qg_ref, rows)
        kg = head_norm(pg[:, GK0:GK0 + LANES], gk_ref[...])
        vg = pg[:, GV0:GV0 + LANES]
        if ctx:
            kg32_ref[rows, :] = kg
            vg32_ref[rows, :] = vg
        kg_ref[rows, :] = rot(kg, rows).astype(BF16)
        vgt_ref[:, rows] = vg.T.astype(BF16)

    def ret_epilogue(pr, rows):
        r_ref[rows, 0:256] = pr[:, 0:256].astype(BF16)
        r_ref[rows, 256:512] = (pr[:, 256:512] * QK_SCALE).astype(BF16)
        r_ref[rows, 512:1024] = pr[:, 512:1024].astype(BF16)

    def win_epilogue(pw, rows):
        store_q([rot(pw[:, s * LANES:(s + 1) * LANES], rows) for s in range(3)], qw_ref, rows)
        kw = pw[:, WK0 - WQ0:WK0 - WQ0 + LANES]
        vw = pw[:, WV0 - WQ0:WV0 - WQ0 + LANES]
        if ctx:
            kw32_ref[rows, :] = kw
            vw32_ref[rows, :] = vw
        kw_ref[rows, :] = rot(kw, rows).astype(BF16)
        vwt_ref[:, rows] = vw.T.astype(BF16)

    tiles = [slice(t * sub, (t + 1) * sub) for t in range(IN_SUBTILES)]
    hb = prologue(tiles[0])
    late = None
    for t, rows in enumerate(tiles):
        pg = _dot(hb, w_ref[:, GQ0:RET0])
        if late is not None:
            late()
        hb_next = prologue(tiles[t + 1]) if t + 1 < IN_SUBTILES else None
        pw = _dot(hb, w_ref[:, WQ0:IN_WIDTH])
        glob_epilogue(pg, rows)
        pr = _dot(hb, w_ref[:, RET0:WQ0])
        win_epilogue(pw, rows)
        late = functools.partial(ret_epilogue, pr, rows)
        hb = hb_next
    late()


def _in_proj(x2d, mod, gpre, w_in_b, gq_t, gk_t, bd, rope_tabs, *, layer, ctx, rows_per_batch):
    t_rows = x2d.shape[0]
    tm = TM_IN
    tiles_per_batch = rows_per_batch // tm
    rope = rope_tabs is not None
    const2 = lambda i: (0, 0)
    lay3 = lambda i: (layer, 0, 0)
    row = lambda i: (i, 0)
    col = lambda i: (0, i)
    in_specs = [pl.BlockSpec((tm, D_MODEL), row),
                _mod_spec(layer, ctx, tiles_per_batch),
                pl.BlockSpec((None, 1, D_MODEL), lay3),
                _resident((None, D_MODEL, IN_WIDTH), lay3),
                pl.BlockSpec((None, 1, LANES), lay3),
                pl.BlockSpec((None, 1, LANES), lay3),
                pl.BlockSpec((LANES, LANES), const2)]
    args = [x2d, mod, gpre, w_in_b, gq_t, gk_t, bd]
    if rope:
        tab = pl.BlockSpec((tm, LANES), lambda i: (i % tiles_per_batch, 0))
        in_specs += [tab, tab, tab]
        args += list(rope_tabs)
    bf = lambda shape: jax.ShapeDtypeStruct(shape, BF16)
    out_shape = [bf((t_rows, GLOB_HEADS * LANES)), bf((t_rows, LANES)), bf((LANES, t_rows)),
                 bf((t_rows, 4 * RET_HEADS * HEAD_DIM)),
                 bf((t_rows, WIN_HEADS * LANES)), bf((t_rows, LANES)), bf((LANES, t_rows))]
    out_specs = [pl.BlockSpec((tm, GLOB_HEADS * LANES), row), pl.BlockSpec((tm, LANES), row),
                 pl.BlockSpec((LANES, tm), col),
                 pl.BlockSpec((tm, 4 * RET_HEADS * HEAD_DIM), row),
                 pl.BlockSpec((tm, WIN_HEADS * LANES), row), pl.BlockSpec((tm, LANES), row),
                 pl.BlockSpec((LANES, tm), col)]
    if ctx:
        out_shape += [jax.ShapeDtypeStruct((t_rows, LANES), F32)] * 4
        out_specs += [pl.BlockSpec((tm, LANES), row)] * 4
    return pl.pallas_call(
        functools.partial(_in_proj_kernel, rope=rope, ctx=ctx),
        grid=(t_rows // tm,),
        in_specs=in_specs, out_specs=out_specs, out_shape=out_shape,
        compiler_params=_cparams(("parallel",)),
        name="in_proj_ctx" if ctx else "in_proj_smp",
    )(*args)


def _attn_kernel(*refs, kinds, sink_layer, tq, n_lat):
    q_ref = refs[0]
    pos = 1
    srcs = []
    for kind in kinds:
        cnt = 2 if kind == "full" else 6
        srcs.append((kind, refs[pos:pos + cnt]))
        pos += cnt
    if sink_layer is not None:
        sink_ref = refs[pos]
        pos += 1
    o_ref = refs[pos]
    j = pl.program_id(1)
    width = tq + 2 * WINDOW
    cols = GROUP * tq
    col_q = lax.broadcasted_iota(jnp.int32, (1, cols), 1)
    if "band" in kinds:
        krel = lax.broadcasted_iota(jnp.int32, (width, cols), 0) - WINDOW
        qrel = lax.broadcasted_iota(jnp.int32, (width, cols), 1) & (tq - 1)
        kpos = krel + j * tq
        band_ok = (jnp.abs(krel - qrel) <= WINDOW) & (kpos >= 0) & (kpos < n_lat)

    ones = lambda w: jnp.ones((ONES_ROWS, w), BF16)
    keys = []
    for kind, r in srcs:
        keys.append(r[0][...] if kind == "full" else jnp.concatenate([r[0][...], r[1][...], r[2][...]], axis=0))

    def values(kv):
        half = slice(HEAD_DIM * kv, HEAD_DIM * (kv + 1))
        out = []
        for kind, r in srcs:
            if kind == "full":
                vt = r[1][half, :]
            else:
                vt = jnp.concatenate([r[3][half, :], r[4][half, :], r[5][half, :]], axis=1)
            out.append(jnp.concatenate([vt, ones(vt.shape[1])], axis=0))
        return out

    def scores(kv):
        qs = jnp.concatenate(
            [q_ref[:, (GROUP * kv + g) * LANES:(GROUP * kv + g + 1) * LANES] for g in range(GROUP)], axis=0)
        sts = []
        for (kind, _), k in zip(srcs, keys):
            st = _dot_nt(k, qs)
            sts.append(jnp.where(band_ok, st, NEG) if kind == "band" else st)
        return sts

    def finish(kv, sts):
        m = sts[0].max(axis=0, keepdims=True)
        for st in sts[1:]:
            m = jnp.maximum(m, st.max(axis=0, keepdims=True))
        if sink_layer is not None:
            s0 = sink_ref[sink_layer, GROUP * kv] * LOG2E
            s1 = sink_ref[sink_layer, GROUP * kv + 1] * LOG2E
            s2 = sink_ref[sink_layer, GROUP * kv + 2] * LOG2E
            sk = jnp.where(col_q < tq, s0, jnp.where(col_q < 2 * tq, s1, s2))
            m = jnp.maximum(m, sk)
        acc = jnp.zeros((HEAD_DIM + ONES_ROWS, cols), F32)
        for st, vt in zip(sts, values(kv)):
            acc = acc + _dot(vt, jnp.exp2(st - m).astype(BF16))
        den = acc[HEAD_DIM:HEAD_DIM + 1, :]
        if sink_layer is not None:
            den = den + jnp.exp2(sk - m)
        o = acc[:HEAD_DIM, :] / den
        return [o[:, g * tq:(g + 1) * tq] for g in range(GROUP)]

    pieces = []
    pending = scores(0)
    for kv in range(KV_HEADS):
        following = scores(kv + 1) if kv + 1 < KV_HEADS else None
        pieces += finish(kv, pending)
        pending = following

    for slab in range(GLOB_HEADS // 2):
        t = jnp.concatenate([pieces[2 * slab], pieces[2 * slab + 1]], axis=0)
        o_ref[:, slab * LANES:(slab + 1) * LANES] = t.T.astype(BF16)


def _attention(q, sources, kinds, sink, sink_layer, *, bsz, n, name):
    tq = TQ
    nq = n // tq
    in_specs = [pl.BlockSpec((tq, q.shape[1]), lambda b, j: (b * nq + j, 0))]
    args = [q]
    for arr, spec in sources:
        in_specs.append(spec)
        args.append(arr)
    if sink is not None:
        in_specs.append(pl.BlockSpec(memory_space=pltpu.SMEM))
        args.append(sink)
    ow = GLOB_HEADS * HEAD_DIM
    return pl.pallas_call(
        functools.partial(_attn_kernel, kinds=kinds, sink_layer=sink_layer if sink is not None else None,
                          tq=tq, n_lat=n),
        grid=(bsz, nq),
        in_specs=in_specs,
        out_specs=pl.BlockSpec((tq, ow), lambda b, j: (b * nq + j, 0)),
        out_shape=jax.ShapeDtypeStruct((bsz * n, ow), BF16),
        compiler_params=_cparams(("parallel", "parallel")),
        name=name,
    )(*args)


def _full_source(k, vt, n):
    return [(k, pl.BlockSpec((n, LANES), lambda b, j: (b, 0))),
            (vt, pl.BlockSpec((LANES, n), lambda b, j: (0, b)))]


def _cache_source(k_all, vt_all, layer, past):
    return [(k_all, pl.BlockSpec((None, past, LANES), lambda b, j: (layer, b, 0))),
            (vt_all, pl.BlockSpec((None, LANES, past), lambda b, j: (layer, 0, b)))]


def _band_source(k, vt, n):
    tq = TQ
    per = n // WINDOW
    step = tq // WINDOW
    prev = lambda b, j: b * per + jnp.maximum(step * j - 1, 0)
    nxt = lambda b, j: b * per + jnp.minimum(step * j + step, per - 1)
    mid = lambda b, j: b * (n // tq) + j
    return [(k, pl.BlockSpec((WINDOW, LANES), lambda b, j: (prev(b, j), 0))),
            (k, pl.BlockSpec((tq, LANES), lambda b, j: (mid(b, j), 0))),
            (k, pl.BlockSpec((WINDOW, LANES), lambda b, j: (nxt(b, j), 0))),
            (vt, pl.BlockSpec((LANES, WINDOW), lambda b, j: (0, prev(b, j)))),
            (vt, pl.BlockSpec((LANES, tq), lambda b, j: (0, mid(b, j)))),
            (vt, pl.BlockSpec((LANES, WINDOW), lambda b, j: (0, nxt(b, j))))]


def _ret_kernel(*refs, n, chunk, has_init, emit_state):
    r_ref, dec_ref, bd_ref = refs[:3]
    pos = 3
    if has_init:
        s0f_ref, s0b_ref = refs[pos:pos + 2]
        pos += 2
    o_ref = refs[pos]
    pos += 1
    if emit_state:
        sf_ref, sb_ref = refs[pos:pos + 2]
        pos += 2
    stf, stb, dsum, dec, cdec = refs[pos:pos + 5]
    nc = n // chunk
    n_slab = RET_HEADS // 2
    use_inter = has_init or nc > 1
    lane_lo = lax.broadcasted_iota(jnp.int32, (chunk, LANES), 1) < HEAD_DIM
    row_s = lax.broadcasted_iota(jnp.int32, (LANES, LANES), 0) < HEAD_DIM
    lane_s = lax.broadcasted_iota(jnp.int32, (LANES, LANES), 1) < HEAD_DIM
    blockdiag = row_s == lane_s

    @pl.when(pl.program_id(0) == 0)
    def _():
        x = dec_ref[...]
        lg = jnp.minimum(x, 0.0) - jnp.log(1.0 + jnp.exp(-jnp.abs(x)))
        rowc = lax.broadcasted_iota(jnp.int32, (chunk, LANES), 0).astype(F32)
        diff = (lax.broadcasted_iota(jnp.int32, (chunk, chunk), 0)
                - lax.broadcasted_iota(jnp.int32, (chunk, chunk), 1)).astype(F32)
        for hd in range(RET_HEADS):
            lf = lg[hd:hd + 1, 0:1]
            lb = lg[RET_HEADS + hd:RET_HEADS + hd + 1, 0:1]
            dsum[hd] = (jnp.where(diff >= 0, jnp.exp(jnp.maximum(diff, 0.0) * lf), 0.0)
                        + jnp.where(diff <= 0, jnp.exp(jnp.maximum(-diff, 0.0) * lb), 0.0))
        for sl in range(n_slab):
            f0, f1 = lg[2 * sl:2 * sl + 1], lg[2 * sl + 1:2 * sl + 2]
            b0 = lg[RET_HEADS + 2 * sl:RET_HEADS + 2 * sl + 1]
            b1 = lg[RET_HEADS + 2 * sl + 1:RET_HEADS + 2 * sl + 2]
            lf2 = jnp.where(lane_lo, f0, f1)
            lb2 = jnp.where(lane_lo, b0, b1)
            dec[0 + sl] = jnp.exp((chunk - 1.0 - rowc) * lf2)
            dec[2 + sl] = jnp.exp(rowc * lb2)
            dec[4 + sl] = jnp.exp((rowc + 1.0) * lf2)
            dec[6 + sl] = jnp.exp((chunk - rowc) * lb2)
            cdec[sl] = jnp.exp(chunk * jnp.where(row_s, f0, f1))
            cdec[n_slab + sl] = jnp.exp(chunk * jnp.where(row_s, b0, b1))

    lo, hi = slice(0, HEAD_DIM), slice(HEAD_DIM, LANES)
    for sl in range(n_slab):
        stf[0, sl] = jnp.zeros((LANES, LANES), F32)
        stb[nc, sl] = jnp.zeros((LANES, LANES), F32)
        if has_init:
            stf[0, sl, lo, lo] = s0f_ref[0, 2 * sl]
            stf[0, sl, hi, hi] = s0f_ref[0, 2 * sl + 1]
            stb[nc, sl, lo, lo] = s0b_ref[0, 2 * sl]
            stb[nc, sl, hi, hi] = s0b_ref[0, 2 * sl + 1]

    def local_state(ci, carry):
        base = pl.multiple_of(ci * chunk, chunk)
        for sl in range(n_slab):
            k2 = r_ref[pl.ds(base, chunk), 256 + sl * LANES:256 + (sl + 1) * LANES].astype(F32)
            v2 = r_ref[pl.ds(base, chunk), 512 + sl * LANES:512 + (sl + 1) * LANES]
            kk = jnp.concatenate([(k2 * dec[0 + sl]).astype(BF16), (k2 * dec[2 + sl]).astype(BF16)], axis=1)
            u = _dot_tn(kk, v2)
            stf[ci + 1, sl] = jnp.where(blockdiag, u[:LANES], 0.0)
            stb[ci, sl] = jnp.where(blockdiag, u[LANES:], 0.0)
        return carry

    lax.fori_loop(0, nc, local_state, 0)
    for sl in range(n_slab):
        for ci in range(nc):
            stf[ci + 1, sl] = stf[ci, sl] * cdec[sl] + stf[ci + 1, sl]
        for ci in range(nc - 1, -1, -1):
            stb[ci, sl] = stb[ci + 1, sl] * cdec[n_slab + sl] + stb[ci, sl]
        if emit_state:
            sf_ref[0, 2 * sl] = stf[nc, sl, lo, lo]
            sf_ref[0, 2 * sl + 1] = stf[nc, sl, hi, hi]
            sb_ref[0, 2 * sl] = stb[0, sl, lo, lo]
            sb_ref[0, 2 * sl + 1] = stb[0, sl, hi, hi]

    def outputs(ci, carry):
        base = pl.multiple_of(ci * chunk, chunk)
        for sl in range(n_slab):
            cols = lambda off: slice(off + sl * LANES, off + (sl + 1) * LANES)
            q2 = r_ref[pl.ds(base, chunk), cols(0)]
            k2 = r_ref[pl.ds(base, chunk), cols(256)]
            v2 = r_ref[pl.ds(base, chunk), cols(512)]
            g2 = r_ref[pl.ds(base, chunk), cols(768)].astype(F32)
            zero = jnp.zeros_like(q2)
            a_lo = (_dot_nt(jnp.where(lane_lo, q2, zero), k2) * dsum[2 * sl]).astype(BF16)
            a_hi = (_dot_nt(jnp.where(lane_lo, zero, q2), k2) * dsum[2 * sl + 1]).astype(BF16)
            o2 = jnp.where(lane_lo, _dot(a_lo, v2), _dot(a_hi, v2))
            if use_inter:
                st = jnp.concatenate([stf[ci, sl], stb[ci + 1, sl]], axis=1).astype(BF16)
                it = _dot(q2, st)
                o2 = o2 + it[:, :LANES] * dec[4 + sl] + it[:, LANES:] * dec[6 + sl]
            ss = _dot((o2 * o2).astype(BF16), bd_ref[...])
            y = o2 * lax.rsqrt(ss + EPS) * _silu(g2)
            o_ref[pl.ds(base, chunk), sl * LANES:(sl + 1) * LANES] = y.astype(BF16)
        return carry

    lax.fori_loop(0, nc, outputs, 0)


def _retention(r, dec_tab, bd, init, *, layer, bsz, n, emit_state, name):
    rw = r.shape[1]
    chunk = RET_CHUNK
    nc = n // chunk
    n_slab = RET_HEADS // 2
    has_init = init is not None
    in_specs = [pl.BlockSpec((n, rw), lambda b: (b, 0)),
                pl.BlockSpec((None, 8, LANES), lambda b: (layer, 0, 0)),
                pl.BlockSpec((LANES, LANES), lambda b: (0, 0))]
    args = [r, dec_tab, bd]
    if has_init:
        init_spec = pl.BlockSpec((1, None, RET_HEADS, HEAD_DIM, HEAD_DIM), lambda b: (b, layer, 0, 0, 0))
        in_specs += [init_spec, init_spec]
        args += list(init)
    ow = RET_HEADS * HEAD_DIM
    out_shape = [jax.ShapeDtypeStruct((bsz * n, ow), BF16)]
    out_specs = [pl.BlockSpec((n, ow), lambda b: (b, 0))]
    if emit_state:
        st_spec = pl.BlockSpec((1, RET_HEADS, HEAD_DIM, HEAD_DIM), lambda b: (b, 0, 0, 0))
        out_shape += [jax.ShapeDtypeStruct((bsz, RET_HEADS, HEAD_DIM, HEAD_DIM), F32)] * 2
        out_specs += [st_spec, st_spec]
    return pl.pallas_call(
        functools.partial(_ret_kernel, n=n, chunk=chunk, has_init=has_init, emit_state=emit_state),
        grid=(bsz,),
        in_specs=in_specs, out_specs=out_specs, out_shape=out_shape,
        scratch_shapes=[pltpu.VMEM((nc + 1, n_slab, LANES, LANES), F32),
                        pltpu.VMEM((nc + 1, n_slab, LANES, LANES), F32),
                        pltpu.VMEM((RET_HEADS, chunk, chunk), F32),
                        pltpu.VMEM((8, chunk, LANES), F32),
                        pltpu.VMEM((2 * n_slab, LANES, LANES), F32)],
        compiler_params=_cparams(("arbitrary",)),
        name=name,
    )(*args)


def _post_kernel(x_ref, og_ref, or_ref, ow_ref, mod_ref, wo_ref, wgu_ref, wd_ref,
                 gpost_ref, gpre2_ref, gpost2_ref, o_ref):
    m = mod_ref[0]
    gt1 = m[:, 2 * D_MODEL:3 * D_MODEL]
    sh2 = m[:, 3 * D_MODEL:4 * D_MODEL]
    sc2 = m[:, 4 * D_MODEL:5 * D_MODEL]
    gt2 = m[:, 5 * D_MODEL:6 * D_MODEL]
    sub = x_ref.shape[0] // POST_SUBTILES
    tiles = [slice(t * sub, (t + 1) * sub) for t in range(POST_SUBTILES)]

    def out_proj(rows):
        merged = jnp.concatenate([og_ref[rows, :], or_ref[rows, :], ow_ref[rows, :]], axis=1)
        return _dot(merged, wo_ref[...])

    def norms(rows, mixed):
        x1 = x_ref[rows, :] + gt1 * (_row_rms(mixed) * gpost_ref[...])
        return x1, (_row_rms(x1) * (gpre2_ref[...] * (1.0 + sc2)) + sh2).astype(BF16)

    def ffn(h2):
        ff = None
        for lo, hi in FF_CHUNKS:
            act = (_silu(_dot(h2, wgu_ref[:, lo:hi])) * _dot(h2, wgu_ref[:, D_FF + lo:D_FF + hi])).astype(BF16)
            part = _dot(act, wd_ref[lo:hi, :])
            ff = part if ff is None else ff + part
        return ff

    mixed = [out_proj(rows) for rows in tiles]
    x1s, ffs = [], []
    for rows, mx in zip(tiles, mixed):
        x1, h2 = norms(rows, mx)
        x1s.append(x1)
        ffs.append(ffn(h2))
    for rows, x1, ff in zip(tiles, x1s, ffs):
        o_ref[rows, :] = x1 + gt2 * (_row_rms(ff) * gpost2_ref[...])


def _post(x2d, og, orr, ow, mod, w_out_b, w_gu_b, w_down_b, gpost, gpre2, gpost2, *, layer, ctx, rows_per_batch):
    t_rows = x2d.shape[0]
    tm = TM_POST
    row = lambda i: (i, 0)
    lay3 = lambda i: (layer, 0, 0)
    gain = pl.BlockSpec((None, 1, D_MODEL), lay3)
    return pl.pallas_call(
        _post_kernel,
        grid=(t_rows // tm,),
        in_specs=[pl.BlockSpec((tm, D_MODEL), row),
                  pl.BlockSpec((tm, og.shape[1]), row),
                  pl.BlockSpec((tm, orr.shape[1]), row),
                  pl.BlockSpec((tm, ow.shape[1]), row),
                  _mod_spec(layer, ctx, rows_per_batch // tm),
                  _resident((None, D_MODEL, D_MODEL), lay3),
                  _resident((None, D_MODEL, 2 * D_FF), lay3),
                  _resident((None, D_FF, D_MODEL), lay3),
                  gain, gain, gain],
        out_specs=pl.BlockSpec((tm, D_MODEL), row),
        out_shape=jax.ShapeDtypeStruct((t_rows, D_MODEL), F32),
        compiler_params=_cparams(("parallel",)),
        name="post_mixer",
    )(x2d, og, orr, ow, mod, w_out_b, w_gu_b, w_down_b, gpost, gpre2, gpost2)


def _rope_tables(n):
    half = HEAD_DIM // 4
    pos = jnp.arange(n, dtype=jnp.int32)
    row = (pos // GRID_W).astype(F32)
    col = (pos % GRID_W).astype(F32)
    freqs = ROPE_BASE ** (-jnp.arange(half, dtype=F32) / half)
    ang_r = row[:, None] * freqs[None, :]
    ang_c = col[:, None] * freqs[None, :]
    ang = jnp.concatenate([ang_r, ang_r, ang_c, ang_c], axis=-1)
    second = (jnp.arange(HEAD_DIM) % (2 * half)) >= half
    cos = jnp.cos(ang)
    sin = jnp.sin(ang)
    sin_a = jnp.where(second[None, :], sin, 0.0)
    sin_b = jnp.where(second[None, :], 0.0, -sin)
    tile = lambda t: jnp.concatenate([t, t], axis=-1)
    return tile(cos), tile(sin_a), tile(sin_b)


def _cache_layouts(ck, cv):
    b, _, past = ck.shape[:3]
    k = jnp.transpose(ck.reshape(b, DEPTH, past, LANES), (1, 0, 2, 3)).reshape(DEPTH, b * past, LANES)
    vt = jnp.transpose(cv.reshape(b, DEPTH, past, LANES), (1, 3, 0, 2)).reshape(DEPTH, LANES, b * past)
    return k.astype(BF16), vt.astype(BF16)


def kernel(x_prompt, x_sample, cache_glob_k, cache_glob_v, state_ret_fwd, state_ret_bwd, cache_win_k, cache_win_v, c, c_ctx, w_mod, b_mod, g_pre_mix, g_post_mix, g_pre_ffn, g_post_ffn, w_in, g_q, g_k, ret_decay_fwd, ret_decay_bwd, win_sink, w_out, w_gate_up, w_down):
    bc, nc_, _ = x_prompt.shape
    bs, ns, _ = x_sample.shape
    past = cache_glob_k.shape[2]

    cond8 = jnp.concatenate([c, c_ctx[None, :], jnp.zeros((MOD_ROWS - bs - 1, D_MODEL), F32)], axis=0)
    mod = _modulation(cond8, w_mod, b_mod).reshape(DEPTH * MOD_ROWS, 1, 6 * D_MODEL)

    w_in_b = w_in.astype(BF16)
    w_out_b = w_out.astype(BF16)
    w_gu_b = w_gate_up.astype(BF16)
    w_down_b = w_down.astype(BF16)
    rope_tabs = _rope_tables(ns)
    seg = jnp.arange(LANES) // HEAD_DIM
    bd = jnp.where(seg[:, None] == seg[None, :], 1.0 / HEAD_DIM, 0.0).astype(BF16)
    gain3 = lambda g: g.reshape(DEPTH, 1, D_MODEL)
    gpre, gpost, gpre2, gpost2 = gain3(g_pre_mix), gain3(g_post_mix), gain3(g_pre_ffn), gain3(g_post_ffn)
    gq_t = jnp.tile(g_q, (1, 2)).reshape(DEPTH, 1, LANES)
    gk_t = jnp.tile(g_k, (1, 2)).reshape(DEPTH, 1, LANES)
    dec_tab = jnp.broadcast_to(jnp.concatenate([ret_decay_fwd, ret_decay_bwd], axis=1)[:, :, None],
                               (DEPTH, 2 * RET_HEADS, LANES))
    init = (state_ret_fwd, state_ret_bwd)
    cgk, cgvt = _cache_layouts(cache_glob_k, cache_glob_v)
    cwk, cwvt = _cache_layouts(cache_win_k, cache_win_v)

    y = x_prompt.reshape(bc * nc_, D_MODEL)
    z = x_sample.reshape(bs * ns, D_MODEL)
    new = [[] for _ in range(6)]
    for l in range(DEPTH):
        qg, kg, vgt, r, qw, kw, vwt, kg32, vg32, kw32, vw32 = _in_proj(
            y, mod, gpre, w_in_b, gq_t, gk_t, bd, None, layer=l, ctx=True, rows_per_batch=nc_)
        og = _attention(qg, _full_source(kg, vgt, nc_), ("full",), None, l,
                        bsz=bc, n=nc_, name="attn_glob_ctx")
        ow = _attention(qw, _full_source(kw, vwt, nc_), ("full",), win_sink, l,
                        bsz=bc, n=nc_, name="attn_win_ctx")
        orr, sf, sb = _retention(r, dec_tab, bd, None, layer=l, bsz=bc, n=nc_, emit_state=True, name="ret_ctx")
        y = _post(y, og, orr, ow, mod, w_out_b, w_gu_b, w_down_b, gpost, gpre2, gpost2,
                  layer=l, ctx=True, rows_per_batch=nc_)
        for lst, t in zip(new, (kg32, vg32, sf, sb, kw32, vw32)):
            lst.append(t)

        qg, kg, vgt, r, qw, kw, vwt = _in_proj(
            z, mod, gpre, w_in_b, gq_t, gk_t, bd, rope_tabs, layer=l, ctx=False, rows_per_batch=ns)
        og = _attention(qg, _full_source(kg, vgt, ns) + _cache_source(cgk, cgvt, l, past),
                        ("full", "full"), None, l, bsz=bs, n=ns, name="attn_glob_smp")
        ow = _attention(qw, _band_source(kw, vwt, ns) + _cache_source(cwk, cwvt, l, past),
                        ("band", "full"), win_sink, l, bsz=bs, n=ns, name="attn_win_smp")
        (orr,) = _retention(r, dec_tab, bd, init, layer=l, bsz=bs, n=ns, emit_state=False, name="ret_smp")
        z = _post(z, og, orr, ow, mod, w_out_b, w_gu_b, w_down_b, gpost, gpre2, gpost2,
                  layer=l, ctx=False, rows_per_batch=ns)

    kv5 = lambda ts: jnp.stack([t.reshape(bc, nc_, LANES) for t in ts], axis=1).reshape(
        bc, DEPTH, nc_, KV_HEADS, HEAD_DIM)
    st5 = lambda ts: jnp.stack(ts, axis=1)
    return (y.reshape(bc, nc_, D_MODEL), z.reshape(bs, ns, D_MODEL),
            kv5(new[0]), kv5(new[1]), st5(new[2]), st5(new[3]), kv5(new[4]), kv5(new[5]))
```

```python
import functools

import jax
import jax.numpy as jnp
from jax import lax
from jax.experimental import pallas as pl
from jax.experimental.pallas import tpu as pltpu

F32 = jnp.float32
BF16 = jnp.bfloat16

D_MODEL = 1024
DEPTH = 4
GRID_W = 64
HEAD_DIM = 64
LANES = 128
GLOB_HEADS = 6
RET_HEADS = 4
WIN_HEADS = 6
KV_HEADS = 2
GROUP = GLOB_HEADS // KV_HEADS
WINDOW = 128
ROPE_BASE = 10000.0
QK_SCALE = HEAD_DIM ** -0.5
LOG2E = 1.4426950408889634
D_FF = 2816
IN_WIDTH = 2304
EPS = 1e-6
NEG = -1e30
MOD_ROWS = 8
CTX_ROW = 4

GQ0, GK0, GV0 = 0, 384, 512
RET0 = 640
WQ0, WK0, WV0 = 1664, 2048, 2176

VMEM_LIMIT = 56 * 1024 * 1024

TM_IN = 512
IN_SUBTILES = 2
TM_POST = 512
POST_SUBTILES = 2
FF_CHUNKS = ((0, 1024), (1024, 2048), (2048, D_FF))
RET_CHUNK = 256
CTX_BPB = 4
TQ_GLOB = 1024
TQ_WIN = 256
ONES_ROWS = 16


def _cparams(sem):
    return pltpu.CompilerParams(dimension_semantics=sem, vmem_limit_bytes=VMEM_LIMIT)


def _resident(block_shape, index_map):
    return pl.BlockSpec(block_shape, index_map, pipeline_mode=pl.Buffered(1))


def _dot(a, b):
    return jnp.dot(a, b, preferred_element_type=F32)


def _dot_nt(a, b):
    return lax.dot_general(a, b, (((1,), (1,)), ((), ())), preferred_element_type=F32)


def _dot_tn(a, b):
    return lax.dot_general(a, b, (((0,), (0,)), ((), ())), preferred_element_type=F32)


def _silu(x):
    return x / (1.0 + jnp.exp(-x))


def _row_rms(x):
    return x * lax.rsqrt(jnp.mean(x * x, axis=-1, keepdims=True) + EPS)


def _chunk_rows(offset, ci, chunk):
    if isinstance(ci, int):
        return pl.ds(offset + ci * chunk, chunk)
    return pl.ds(pl.multiple_of(offset + ci * chunk, chunk), chunk)


def _run_skewed(chains):
    chains = list(chains)
    done = [False] * len(chains)
    t = 0
    while not all(done):
        live = [c for c in range(min(t + 1, len(chains))) if not done[c]]
        for c in sorted(live, key=lambda c: (t - c) % 2):
            try:
                next(chains[c])
            except StopIteration:
                done[c] = True
        t += 1


def _mod_kernel(c_ref, w_ref, b_ref, o_ref):
    s = _silu(c_ref[...]).astype(BF16)
    o_ref[0] = _dot(s, w_ref[0].astype(BF16)) + b_ref[0]


def _modulation(cond8, w_mod, b_mod):
    tn = 1536
    n_out = 6 * D_MODEL
    return pl.pallas_call(
        _mod_kernel,
        grid=(DEPTH, n_out // tn),
        in_specs=[pl.BlockSpec((MOD_ROWS, D_MODEL), lambda l, j: (0, 0)),
                  pl.BlockSpec((1, D_MODEL, tn), lambda l, j: (l, 0, j)),
                  pl.BlockSpec((1, 1, tn), lambda l, j: (l, 0, j))],
        out_specs=pl.BlockSpec((1, MOD_ROWS, tn), lambda l, j: (l, 0, j)),
        out_shape=jax.ShapeDtypeStruct((DEPTH, MOD_ROWS, n_out), F32),
        compiler_params=_cparams(("parallel", "parallel")),
        name="modulation",
    )(cond8, w_mod, b_mod.reshape(DEPTH, 1, n_out))


def _mod_spec(layer, ctx, tiles_per_batch):
    if ctx:
        return pl.BlockSpec((1, 1, 6 * D_MODEL), lambda i: (layer * MOD_ROWS + CTX_ROW, 0, 0))
    return pl.BlockSpec((1, 1, 6 * D_MODEL), lambda i: (layer * MOD_ROWS + i // tiles_per_batch, 0, 0))


def _in_proj_kernel(*refs, rope, ctx):
    x_ref, mod_ref, gpre_ref, w_ref, gq_ref, gk_ref, bd_ref = refs[:7]
    pos = 7
    if rope:
        cos_ref, sa_ref, sb_ref = refs[pos:pos + 3]
        pos += 3
    qg_ref, kg_ref, vgt_ref, r_ref, qw_ref, kw_ref, vwt_ref = refs[pos:pos + 7]
    pos += 7
    if ctx:
        kg32_ref, vg32_ref, kw32_ref, vw32_ref = refs[pos:pos + 4]

    m = mod_ref[0]
    sh1 = m[:, 0:D_MODEL]
    sc1 = m[:, D_MODEL:2 * D_MODEL]
    sub = x_ref.shape[0] // IN_SUBTILES
    lane_lo = lax.broadcasted_iota(jnp.int32, (sub, LANES), 1) < HEAD_DIM

    def prologue(rows):
        return (_row_rms(x_ref[rows, :]) * (gpre_ref[...] * (1.0 + sc1)) + sh1).astype(BF16)

    def head_norm(t, g):
        ss = _dot((t * t).astype(BF16), bd_ref[...])
        return t * lax.rsqrt(ss + EPS) * g

    def rot(t, rows):
        if not rope:
            return t
        return (t * cos_ref[rows, :] + pltpu.roll(t, 16, 1) * sa_ref[rows, :]
                + pltpu.roll(t, LANES - 16, 1) * sb_ref[rows, :])

    def store_q(slabs, q_ref, rows):
        for hd in range(GLOB_HEADS):
            t = slabs[hd // 2]
            kv = hd // GROUP
            if hd % 2 != kv:
                t = pltpu.roll(t, HEAD_DIM, 1)
            keep = lane_lo if kv == 0 else jnp.logical_not(lane_lo)
            q_ref[rows, hd * LANES:(hd + 1) * LANES] = jnp.where(keep, t * (QK_SCALE * LOG2E), 0.0).astype(BF16)

    def glob_epilogue(pg, rows):
        store_q([rot(head_norm(pg[:, s * LANES:(s + 1) * LANES], gq_ref[...]), rows) for s in range(3)],
                qg_ref, rows)
        kg = head_norm(pg[:, GK0:GK0 + LANES], gk_ref[...])
        vg = pg[:, GV0:GV0 + LANES]
        if ctx:
            kg32_ref[rows, :] = kg
            vg32_ref[rows, :] = vg
        kg_ref[rows, :] = rot(kg, rows).astype(BF16)
        vgt_ref[:, rows] = vg.T.astype(BF16)

    def ret_epilogue(pr, rows):
        r_ref[rows, 0:256] = pr[:, 0:256].astype(BF16)
        r_ref[rows, 256:512] = (pr[:, 256:512] * QK_SCALE).astype(BF16)
        r_ref[rows, 512:1024] = pr[:, 512:1024].astype(BF16)

    def win_epilogue(pw, rows):
        store_q([rot(pw[:, s * LANES:(s + 1) * LANES], rows) for s in range(3)], qw_ref, rows)
        kw = pw[:, WK0 - WQ0:WK0 - WQ0 + LANES]
        vw = pw[:, WV0 - WQ0:WV0 - WQ0 + LANES]
        if ctx:
            kw32_ref[rows, :] = kw
            vw32_ref[rows, :] = vw
        kw_ref[rows, :] = rot(kw, rows).astype(BF16)
        vwt_ref[:, rows] = vw.T.astype(BF16)

    tiles = [slice(t * sub, (t + 1) * sub) for t in range(IN_SUBTILES)]
    hb = prologue(tiles[0])
    late = None
    for t, rows in enumerate(tiles):
        pg = _dot(hb, w_ref[:, GQ0:RET0])
        if late is not None:
            late()
        hb_next = prologue(tiles[t + 1]) if t + 1 < IN_SUBTILES else None
        pw = _dot(hb, w_ref[:, WQ0:IN_WIDTH])
        glob_epilogue(pg, rows)
        pr = _dot(hb, w_ref[:, RET0:WQ0])
        win_epilogue(pw, rows)
        late = functools.partial(ret_epilogue, pr, rows)
        hb = hb_next
    late()


def _in_proj(x2d, mod, gpre, w_in_b, gq_t, gk_t, bd, rope_tabs, *, layer, ctx, rows_per_batch):
    t_rows = x2d.shape[0]
    tm = TM_IN
    tiles_per_batch = rows_per_batch // tm
    rope = rope_tabs is not None
    const2 = lambda i: (0, 0)
    lay3 = lambda i: (layer, 0, 0)
    row = lambda i: (i, 0)
    col = lambda i: (0, i)
    in_specs = [pl.BlockSpec((tm, D_MODEL), row),
                _mod_spec(layer, ctx, tiles_per_batch),
                pl.BlockSpec((None, 1, D_MODEL), lay3),
                _resident((None, D_MODEL, IN_WIDTH), lay3),
                pl.BlockSpec((None, 1, LANES), lay3),
                pl.BlockSpec((None, 1, LANES), lay3),
                pl.BlockSpec((LANES, LANES), const2)]
    args = [x2d, mod, gpre, w_in_b, gq_t, gk_t, bd]
    if rope:
        tab = pl.BlockSpec((tm, LANES), lambda i: (i % tiles_per_batch, 0))
        in_specs += [tab, tab, tab]
        args += list(rope_tabs)
    bf = lambda shape: jax.ShapeDtypeStruct(shape, BF16)
    out_shape = [bf((t_rows, GLOB_HEADS * LANES)), bf((t_rows, LANES)), bf((LANES, t_rows)),
                 bf((t_rows, 4 * RET_HEADS * HEAD_DIM)),
                 bf((t_rows, WIN_HEADS * LANES)), bf((t_rows, LANES)), bf((LANES, t_rows))]
    out_specs = [pl.BlockSpec((tm, GLOB_HEADS * LANES), row), pl.BlockSpec((tm, LANES), row),
                 pl.BlockSpec((LANES, tm), col),
                 pl.BlockSpec((tm, 4 * RET_HEADS * HEAD_DIM), row),
                 pl.BlockSpec((tm, WIN_HEADS * LANES), row), pl.BlockSpec((tm, LANES), row),
                 pl.BlockSpec((LANES, tm), col)]
    if ctx:
        out_shape += [jax.ShapeDtypeStruct((t_rows, LANES), F32)] * 4
        out_specs += [pl.BlockSpec((tm, LANES), row)] * 4
    return pl.pallas_call(
        functools.partial(_in_proj_kernel, rope=rope, ctx=ctx),
        grid=(t_rows // tm,),
        in_specs=in_specs, out_specs=out_specs, out_shape=out_shape,
        compiler_params=_cparams(("parallel",)),
        name="in_proj_ctx" if ctx else "in_proj_smp",
    )(*args)


def _attn_kernel(*refs, kinds, sink_layer, tq, n_lat, stack, bpb):
    q_ref = refs[0]
    pos = 1
    srcs = []
    for kind in kinds:
        cnt = 2 if kind == "full" else 6
        srcs.append((kind, refs[pos:pos + cnt]))
        pos += cnt
    if sink_layer is not None:
        sink_ref = refs[pos]
        pos += 1
    o_ref = refs[pos]
    j = pl.program_id(1)
    width = tq + 2 * WINDOW
    cols = stack * tq
    col_q = lax.broadcasted_iota(jnp.int32, (1, cols), 1)
    if "band" in kinds:
        krel = lax.broadcasted_iota(jnp.int32, (width, cols), 0) - WINDOW
        qrel = lax.broadcasted_iota(jnp.int32, (width, cols), 1) & (tq - 1)
        kpos = krel + j * tq
        band_ok = (jnp.abs(krel - qrel) <= WINDOW) & (kpos >= 0) & (kpos < n_lat)

    ones = lambda w: jnp.ones((ONES_ROWS, w), BF16)

    def full_keys(r, bi):
        nk = r[0].shape[0] // bpb
        return slice(bi * nk, (bi + 1) * nk)

    def keys(bi):
        return [r[0][full_keys(r, bi), :] if kind == "full"
                else jnp.concatenate([r[0][...], r[1][...], r[2][...]], axis=0) for kind, r in srcs]

    def values(bi, kv):
        half = slice(HEAD_DIM * kv, HEAD_DIM * (kv + 1))
        out = []
        for kind, r in srcs:
            if kind == "full":
                vt = r[1][half, full_keys(r, bi)]
            else:
                vt = jnp.concatenate([r[3][half, :], r[4][half, :], r[5][half, :]], axis=1)
            out.append(jnp.concatenate([vt, ones(vt.shape[1])], axis=0))
        return out

    def scores(bi, heads, ks):
        rows = slice(bi * tq, (bi + 1) * tq)
        qs = [q_ref[rows, hd * LANES:(hd + 1) * LANES] for hd in heads]
        qs = qs[0] if len(qs) == 1 else jnp.concatenate(qs, axis=0)
        sts = []
        for (kind, _), k in zip(srcs, ks):
            st = _dot_nt(k, qs)
            sts.append(jnp.where(band_ok, st, NEG) if kind == "band" else st)
        return sts

    def finish(heads, vts, sts):
        m = sts[0].max(axis=0, keepdims=True)
        for st in sts[1:]:
            m = jnp.maximum(m, st.max(axis=0, keepdims=True))
        if sink_layer is not None:
            sk = sink_ref[sink_layer, heads[-1]] * LOG2E
            for i in range(len(heads) - 2, -1, -1):
                sk = jnp.where(col_q < (i + 1) * tq, sink_ref[sink_layer, heads[i]] * LOG2E, sk)
            m = jnp.maximum(m, sk)
        acc = jnp.zeros((HEAD_DIM + ONES_ROWS, cols), F32)
        for st, vt in zip(sts, vts):
            acc = acc + _dot(vt, jnp.exp2(st - m).astype(BF16))
        den = acc[HEAD_DIM:HEAD_DIM + 1, :]
        if sink_layer is not None:
            den = den + jnp.exp2(sk - m)
        o = acc[:HEAD_DIM, :] / den
        return [o[:, i * tq:(i + 1) * tq] for i in range(len(heads))]

    items = [(bi, list(range(GROUP * kv + g, GROUP * kv + g + stack)))
             for bi in range(bpb) for kv in range(KV_HEADS) for g in range(0, GROUP, stack)]
    ks = {bi: keys(bi) for bi in range(bpb)}
    vals = {(bi, kv): values(bi, kv) for bi in range(bpb) for kv in range(KV_HEADS)}
    pieces = []
    pending = scores(*items[0], ks[0])
    for i, (bi, heads) in enumerate(items):
        following = scores(*items[i + 1], ks[items[i + 1][0]]) if i + 1 < len(items) else None
        pieces += finish(heads, vals[(bi, heads[0] // GROUP)], pending)
        pending = following

    for bi in range(bpb):
        for slab in range(GLOB_HEADS // 2):
            hd = bi * GLOB_HEADS + 2 * slab
            t = jnp.concatenate([pieces[hd], pieces[hd + 1]], axis=0)
            o_ref[bi * tq:(bi + 1) * tq, slab * LANES:(slab + 1) * LANES] = t.T.astype(BF16)


def _attention(q, sources, kinds, sink, sink_layer, *, bsz, n, tq, stack, name, bpb=1):
    nq = n // tq
    assert bpb == 1 or nq == 1
    rows = bpb * tq
    in_specs = [pl.BlockSpec((rows, q.shape[1]), lambda b, j: (b * nq + j, 0))]
    args = [q]
    for arr, spec in sources:
        in_specs.append(spec)
        args.append(arr)
    if sink is not None:
        in_specs.append(pl.BlockSpec(memory_space=pltpu.SMEM))
        args.append(sink)
    ow = GLOB_HEADS * HEAD_DIM
    return pl.pallas_call(
        functools.partial(_attn_kernel, kinds=kinds, sink_layer=sink_layer if sink is not None else None,
                          tq=tq, n_lat=n, stack=stack, bpb=bpb),
        grid=(bsz // bpb, nq),
        in_specs=in_specs,
        out_specs=pl.BlockSpec((rows, ow), lambda b, j: (b * nq + j, 0)),
        out_shape=jax.ShapeDtypeStruct((bsz * n, ow), BF16),
        compiler_params=_cparams(("parallel", "parallel")),
        name=name,
    )(*args)


def _full_source(k, vt, n):
    return [(k, pl.BlockSpec((n, LANES), lambda b, j: (b, 0))),
            (vt, pl.BlockSpec((LANES, n), lambda b, j: (0, b)))]


def _cache_source(k_all, vt_all, layer, past):
    return [(k_all, pl.BlockSpec((None, past, LANES), lambda b, j: (layer, b, 0))),
            (vt_all, pl.BlockSpec((None, LANES, past), lambda b, j: (layer, 0, b)))]


def _band_source(k, vt, n, tq):
    per = n // WINDOW
    step = tq // WINDOW
    prev = lambda b, j: b * per + jnp.maximum(step * j - 1, 0)
    nxt = lambda b, j: b * per + jnp.minimum(step * j + step, per - 1)
    mid = lambda b, j: b * (n // tq) + j
    return [(k, pl.BlockSpec((WINDOW, LANES), lambda b, j: (prev(b, j), 0))),
            (k, pl.BlockSpec((tq, LANES), lambda b, j: (mid(b, j), 0))),
            (k, pl.BlockSpec((WINDOW, LANES), lambda b, j: (nxt(b, j), 0))),
            (vt, pl.BlockSpec((LANES, WINDOW), lambda b, j: (0, prev(b, j)))),
            (vt, pl.BlockSpec((LANES, tq), lambda b, j: (0, mid(b, j)))),
            (vt, pl.BlockSpec((LANES, WINDOW), lambda b, j: (0, nxt(b, j))))]


def _ret_kernel(*refs, n, chunk, has_init, emit_state, bpb):
    r_ref, dec_ref, bd_ref = refs[:3]
    pos = 3
    if has_init:
        s0f_ref, s0b_ref = refs[pos:pos + 2]
        pos += 2
    o_ref = refs[pos]
    pos += 1
    if emit_state:
        sf_ref, sb_ref = refs[pos:pos + 2]
        pos += 2
    stf, stb, dsum, dec, cdec = refs[pos:pos + 5]
    nc = n // chunk
    n_slab = RET_HEADS // 2
    use_inter = has_init or nc > 1
    lane_lo = lax.broadcasted_iota(jnp.int32, (chunk, LANES), 1) < HEAD_DIM
    row_s = lax.broadcasted_iota(jnp.int32, (LANES, LANES), 0) < HEAD_DIM
    lane_s = lax.broadcasted_iota(jnp.int32, (LANES, LANES), 1) < HEAD_DIM
    blockdiag = row_s == lane_s

    @pl.when(pl.program_id(0) == 0)
    def _():
        x = dec_ref[...]
        lg = jnp.minimum(x, 0.0) - jnp.log(1.0 + jnp.exp(-jnp.abs(x)))
        rowc = lax.broadcasted_iota(jnp.int32, (chunk, LANES), 0).astype(F32)
        diff = (lax.broadcasted_iota(jnp.int32, (chunk, chunk), 0)
                - lax.broadcasted_iota(jnp.int32, (chunk, chunk), 1)).astype(F32)
        for hd in range(RET_HEADS):
            lf = lg[hd:hd + 1, 0:1]
            lb = lg[RET_HEADS + hd:RET_HEADS + hd + 1, 0:1]
            dsum[hd] = (jnp.where(diff >= 0, jnp.exp(jnp.maximum(diff, 0.0) * lf), 0.0)
                        + jnp.where(diff <= 0, jnp.exp(jnp.maximum(-diff, 0.0) * lb), 0.0))
        for sl in range(n_slab):
            f0, f1 = lg[2 * sl:2 * sl + 1], lg[2 * sl + 1:2 * sl + 2]
            b0 = lg[RET_HEADS + 2 * sl:RET_HEADS + 2 * sl + 1]
            b1 = lg[RET_HEADS + 2 * sl + 1:RET_HEADS + 2 * sl + 2]
            lf2 = jnp.where(lane_lo, f0, f1)
            lb2 = jnp.where(lane_lo, b0, b1)
            dec[0 + sl] = jnp.exp((chunk - 1.0 - rowc) * lf2)
            dec[2 + sl] = jnp.exp(rowc * lb2)
            dec[4 + sl] = jnp.exp((rowc + 1.0) * lf2)
            dec[6 + sl] = jnp.exp((chunk - rowc) * lb2)
            cdec[sl] = jnp.exp(chunk * jnp.where(row_s, f0, f1))
            cdec[n_slab + sl] = jnp.exp(chunk * jnp.where(row_s, b0, b1))

    lo, hi = slice(0, HEAD_DIM), slice(HEAD_DIM, LANES)
    for bi in range(bpb):
        for sl in range(n_slab):
            stf[bi, 0, sl] = jnp.zeros((LANES, LANES), F32)
            stb[bi, nc, sl] = jnp.zeros((LANES, LANES), F32)
            if has_init:
                stf[bi, 0, sl, lo, lo] = s0f_ref[bi, 2 * sl]
                stf[bi, 0, sl, hi, hi] = s0f_ref[bi, 2 * sl + 1]
                stb[bi, nc, sl, lo, lo] = s0b_ref[bi, 2 * sl]
                stb[bi, nc, sl, hi, hi] = s0b_ref[bi, 2 * sl + 1]

    def state_chain(bi, ci, sl):
        rows = _chunk_rows(bi * n, ci, chunk)
        yield
        k2 = r_ref[rows, 256 + sl * LANES:256 + (sl + 1) * LANES].astype(F32)
        kk = jnp.concatenate([(k2 * dec[0 + sl]).astype(BF16), (k2 * dec[2 + sl]).astype(BF16)], axis=1)
        yield
        u = _dot_tn(kk, r_ref[rows, 512 + sl * LANES:512 + (sl + 1) * LANES])
        yield
        stf[bi, ci + 1, sl] = jnp.where(blockdiag, u[:LANES], 0.0)
        stb[bi, ci, sl] = jnp.where(blockdiag, u[LANES:], 0.0)

    def output_chain(bi, ci, sl):
        rows = _chunk_rows(bi * n, ci, chunk)
        cols = lambda off: slice(off + sl * LANES, off + (sl + 1) * LANES)
        q2 = r_ref[rows, cols(0)]
        k2 = r_ref[rows, cols(256)]
        zero = jnp.zeros_like(q2)
        s_lo = _dot_nt(jnp.where(lane_lo, q2, zero), k2)
        s_hi = _dot_nt(jnp.where(lane_lo, zero, q2), k2)
        if use_inter:
            it = _dot(q2, jnp.concatenate([stf[bi, ci, sl], stb[bi, ci + 1, sl]], axis=1).astype(BF16))
        yield
        a_lo = (s_lo * dsum[2 * sl]).astype(BF16)
        a_hi = (s_hi * dsum[2 * sl + 1]).astype(BF16)
        yield
        v2 = r_ref[rows, cols(512)]
        o_lo, o_hi = _dot(a_lo, v2), _dot(a_hi, v2)
        yield
        o2 = jnp.where(lane_lo, o_lo, o_hi)
        if use_inter:
            o2 = o2 + it[:, :LANES] * dec[4 + sl] + it[:, LANES:] * dec[6 + sl]
        sq = (o2 * o2).astype(BF16)
        yield
        ss = _dot(sq, bd_ref[...])
        yield
        g2 = r_ref[rows, cols(768)].astype(F32)
        o_ref[rows, sl * LANES:(sl + 1) * LANES] = (o2 * lax.rsqrt(ss + EPS) * _silu(g2)).astype(BF16)

    def scan_states(bi):
        for sl in range(n_slab):
            for ci in range(nc):
                stf[bi, ci + 1, sl] = stf[bi, ci, sl] * cdec[sl] + stf[bi, ci + 1, sl]
            for ci in range(nc - 1, -1, -1):
                stb[bi, ci, sl] = stb[bi, ci + 1, sl] * cdec[n_slab + sl] + stb[bi, ci, sl]
            if emit_state:
                sf_ref[bi, 2 * sl] = stf[bi, nc, sl, lo, lo]
                sf_ref[bi, 2 * sl + 1] = stf[bi, nc, sl, hi, hi]
                sb_ref[bi, 2 * sl] = stb[bi, 0, sl, lo, lo]
                sb_ref[bi, 2 * sl + 1] = stb[bi, 0, sl, hi, hi]

    slabs = range(n_slab)
    if not use_inter:
        _run_skewed([state_chain(bi, 0, sl) for bi in range(bpb) for sl in slabs]
                    + [output_chain(bi, 0, sl) for bi in range(bpb) for sl in slabs])
        for bi in range(bpb):
            scan_states(bi)
    else:
        for bi in range(bpb):
            def local_state(ci, carry):
                _run_skewed([state_chain(bi, ci, sl) for sl in slabs])
                return carry

            def outputs(ci, carry):
                _run_skewed([output_chain(bi, ci, sl) for sl in slabs])
                return carry

            lax.fori_loop(0, nc, local_state, 0)
            scan_states(bi)
            lax.fori_loop(0, nc, outputs, 0)


def _retention(r, dec_tab, bd, init, *, layer, bsz, n, emit_state, name, bpb=1):
    rw = r.shape[1]
    chunk = RET_CHUNK
    nc = n // chunk
    n_slab = RET_HEADS // 2
    has_init = init is not None
    in_specs = [pl.BlockSpec((bpb * n, rw), lambda b: (b, 0)),
                pl.BlockSpec((None, 8, LANES), lambda b: (layer, 0, 0)),
                pl.BlockSpec((LANES, LANES), lambda b: (0, 0))]
    args = [r, dec_tab, bd]
    if has_init:
        init_spec = pl.BlockSpec((bpb, None, RET_HEADS, HEAD_DIM, HEAD_DIM), lambda b: (b, layer, 0, 0, 0))
        in_specs += [init_spec, init_spec]
        args += list(init)
    ow = RET_HEADS * HEAD_DIM
    out_shape = [jax.ShapeDtypeStruct((bsz * n, ow), BF16)]
    out_specs = [pl.BlockSpec((bpb * n, ow), lambda b: (b, 0))]
    if emit_state:
        st_spec = pl.BlockSpec((bpb, RET_HEADS, HEAD_DIM, HEAD_DIM), lambda b: (b, 0, 0, 0))
        out_shape += [jax.ShapeDtypeStruct((bsz, RET_HEADS, HEAD_DIM, HEAD_DIM), F32)] * 2
        out_specs += [st_spec, st_spec]
    return pl.pallas_call(
        functools.partial(_ret_kernel, n=n, chunk=chunk, has_init=has_init, emit_state=emit_state, bpb=bpb),
        grid=(bsz // bpb,),
        in_specs=in_specs, out_specs=out_specs, out_shape=out_shape,
        scratch_shapes=[pltpu.VMEM((bpb, nc + 1, n_slab, LANES, LANES), F32),
                        pltpu.VMEM((bpb, nc + 1, n_slab, LANES, LANES), F32),
                        pltpu.VMEM((RET_HEADS, chunk, chunk), F32),
                        pltpu.VMEM((8, chunk, LANES), F32),
                        pltpu.VMEM((2 * n_slab, LANES, LANES), F32)],
        compiler_params=_cparams(("arbitrary",)),
        name=name,
    )(*args)


def _post_kernel(x_ref, og_ref, or_ref, ow_ref, mod_ref, wo_ref, wgu_ref, wd_ref,
                 gpost_ref, gpre2_ref, gpost2_ref, o_ref):
    m = mod_ref[0]
    gt1 = m[:, 2 * D_MODEL:3 * D_MODEL]
    sh2 = m[:, 3 * D_MODEL:4 * D_MODEL]
    sc2 = m[:, 4 * D_MODEL:5 * D_MODEL]
    gt2 = m[:, 5 * D_MODEL:6 * D_MODEL]
    sub = x_ref.shape[0] // POST_SUBTILES
    tiles = [slice(t * sub, (t + 1) * sub) for t in range(POST_SUBTILES)]

    def out_proj(rows):
        merged = jnp.concatenate([og_ref[rows, :], or_ref[rows, :], ow_ref[rows, :]], axis=1)
        return _dot(merged, wo_ref[...])

    def norms(rows, mixed):
        x1 = x_ref[rows, :] + gt1 * (_row_rms(mixed) * gpost_ref[...])
        return x1, (_row_rms(x1) * (gpre2_ref[...] * (1.0 + sc2)) + sh2).astype(BF16)

    def ffn(h2):
        ff = None
        for lo, hi in FF_CHUNKS:
            act = (_silu(_dot(h2, wgu_ref[:, lo:hi])) * _dot(h2, wgu_ref[:, D_FF + lo:D_FF + hi])).astype(BF16)
            part = _dot(act, wd_ref[lo:hi, :])
            ff = part if ff is None else ff + part
        return ff

    mixed = [out_proj(rows) for rows in tiles]
    x1s, ffs = [], []
    for rows, mx in zip(tiles, mixed):
        x1, h2 = norms(rows, mx)
        x1s.append(x1)
        ffs.append(ffn(h2))
    for rows, x1, ff in zip(tiles, x1s, ffs):
        o_ref[rows, :] = x1 + gt2 * (_row_rms(ff) * gpost2_ref[...])


def _post(x2d, og, orr, ow, mod, w_out_b, w_gu_b, w_down_b, gpost, gpre2, gpost2, *, layer, ctx, rows_per_batch):
    t_rows = x2d.shape[0]
    tm = TM_POST
    row = lambda i: (i, 0)
    lay3 = lambda i: (layer, 0, 0)
    gain = pl.BlockSpec((None, 1, D_MODEL), lay3)
    return pl.pallas_call(
        _post_kernel,
        grid=(t_rows // tm,),
        in_specs=[pl.BlockSpec((tm, D_MODEL), row),
                  pl.BlockSpec((tm, og.shape[1]), row),
                  pl.BlockSpec((tm, orr.shape[1]), row),
                  pl.BlockSpec((tm, ow.shape[1]), row),
                  _mod_spec(layer, ctx, rows_per_batch // tm),
                  _resident((None, D_MODEL, D_MODEL), lay3),
                  _resident((None, D_MODEL, 2 * D_FF), lay3),
                  _resident((None, D_FF, D_MODEL), lay3),
                  gain, gain, gain],
        out_specs=pl.BlockSpec((tm, D_MODEL), row),
        out_shape=jax.ShapeDtypeStruct((t_rows, D_MODEL), F32),
        compiler_params=_cparams(("parallel",)),
        name="post_mixer",
    )(x2d, og, orr, ow, mod, w_out_b, w_gu_b, w_down_b, gpost, gpre2, gpost2)


def _rope_tables(n):
    half = HEAD_DIM // 4
    pos = jnp.arange(n, dtype=jnp.int32)
    row = (pos // GRID_W).astype(F32)
    col = (pos % GRID_W).astype(F32)
    freqs = ROPE_BASE ** (-jnp.arange(half, dtype=F32) / half)
    ang_r = row[:, None] * freqs[None, :]
    ang_c = col[:, None] * freqs[None, :]
    ang = jnp.concatenate([ang_r, ang_r, ang_c, ang_c], axis=-1)
    second = (jnp.arange(HEAD_DIM) % (2 * half)) >= half
    cos = jnp.cos(ang)
    sin = jnp.sin(ang)
    sin_a = jnp.where(second[None, :], sin, 0.0)
    sin_b = jnp.where(second[None, :], 0.0, -sin)
    tile = lambda t: jnp.concatenate([t, t], axis=-1)
    return tile(cos), tile(sin_a), tile(sin_b)


def _cache_layouts(ck, cv):
    b, _, past = ck.shape[:3]
    k = jnp.transpose(ck.reshape(b, DEPTH, past, LANES), (1, 0, 2, 3)).reshape(DEPTH, b * past, LANES)
    vt = jnp.transpose(cv.reshape(b, DEPTH, past, LANES), (1, 3, 0, 2)).reshape(DEPTH, LANES, b * past)
    return k.astype(BF16), vt.astype(BF16)


def kernel(x_prompt, x_sample, cache_glob_k, cache_glob_v, state_ret_fwd, state_ret_bwd, cache_win_k, cache_win_v, c, c_ctx, w_mod, b_mod, g_pre_mix, g_post_mix, g_pre_ffn, g_post_ffn, w_in, g_q, g_k, ret_decay_fwd, ret_decay_bwd, win_sink, w_out, w_gate_up, w_down):
    bc, nc_, _ = x_prompt.shape
    bs, ns, _ = x_sample.shape
    past = cache_glob_k.shape[2]

    cond8 = jnp.concatenate([c, c_ctx[None, :], jnp.zeros((MOD_ROWS - bs - 1, D_MODEL), F32)], axis=0)
    mod = _modulation(cond8, w_mod, b_mod).reshape(DEPTH * MOD_ROWS, 1, 6 * D_MODEL)

    w_in_b = w_in.astype(BF16)
    w_out_b = w_out.astype(BF16)
    w_gu_b = w_gate_up.astype(BF16)
    w_down_b = w_down.astype(BF16)
    rope_tabs = _rope_tables(ns)
    seg = jnp.arange(LANES) // HEAD_DIM
    bd = jnp.where(seg[:, None] == seg[None, :], 1.0 / HEAD_DIM, 0.0).astype(BF16)
    gain3 = lambda g: g.reshape(DEPTH, 1, D_MODEL)
    gpre, gpost, gpre2, gpost2 = gain3(g_pre_mix), gain3(g_post_mix), gain3(g_pre_ffn), gain3(g_post_ffn)
    gq_t = jnp.tile(g_q, (1, 2)).reshape(DEPTH, 1, LANES)
    gk_t = jnp.tile(g_k, (1, 2)).reshape(DEPTH, 1, LANES)
    dec_tab = jnp.broadcast_to(jnp.concatenate([ret_decay_fwd, ret_decay_bwd], axis=1)[:, :, None],
                               (DEPTH, 2 * RET_HEADS, LANES))
    init = (state_ret_fwd, state_ret_bwd)
    cgk, cgvt = _cache_layouts(cache_glob_k, cache_glob_v)
    cwk, cwvt = _cache_layouts(cache_win_k, cache_win_v)

    y = x_prompt.reshape(bc * nc_, D_MODEL)
    z = x_sample.reshape(bs * ns, D_MODEL)
    new = [[] for _ in range(6)]
    for l in range(DEPTH):
        qg, kg, vgt, r, qw, kw, vwt, kg32, vg32, kw32, vw32 = _in_proj(
            y, mod, gpre, w_in_b, gq_t, gk_t, bd, None, layer=l, ctx=True, rows_per_batch=nc_)
        og = _attention(qg, _full_source(kg, vgt, nc_ * CTX_BPB), ("full",), None, l,
                        bsz=bc, n=nc_, tq=nc_, stack=GROUP, bpb=CTX_BPB, name="attn_glob_ctx")
        ow = _attention(qw, _full_source(kw, vwt, nc_ * CTX_BPB), ("full",), win_sink, l,
                        bsz=bc, n=nc_, tq=nc_, stack=GROUP, bpb=CTX_BPB, name="attn_win_ctx")
        orr, sf, sb = _retention(r, dec_tab, bd, None, layer=l, bsz=bc, n=nc_, emit_state=True, bpb=CTX_BPB,
                                 name="ret_ctx")
        y = _post(y, og, orr, ow, mod, w_out_b, w_gu_b, w_down_b, gpost, gpre2, gpost2,
                  layer=l, ctx=True, rows_per_batch=nc_)
        for lst, t in zip(new, (kg32, vg32, sf, sb, kw32, vw32)):
            lst.append(t)

        qg, kg, vgt, r, qw, kw, vwt = _in_proj(
            z, mod, gpre, w_in_b, gq_t, gk_t, bd, rope_tabs, layer=l, ctx=False, rows_per_batch=ns)
        og = _attention(qg, _full_source(kg, vgt, ns) + _cache_source(cgk, cgvt, l, past),
                        ("full", "full"), None, l, bsz=bs, n=ns, tq=TQ_GLOB, stack=1, name="attn_glob_smp")
        ow = _attention(qw, _band_source(kw, vwt, ns, TQ_WIN) + _cache_source(cwk, cwvt, l, past),
                        ("band", "full"), win_sink, l, bsz=bs, n=ns, tq=TQ_WIN, stack=GROUP, name="attn_win_smp")
        (orr,) = _retention(r, dec_tab, bd, init, layer=l, bsz=bs, n=ns, emit_state=False, name="ret_smp")
        z = _post(z, og, orr, ow, mod, w_out_b, w_gu_b, w_down_b, gpost, gpre2, gpost2,
                  layer=l, ctx=False, rows_per_batch=ns)

    kv5 = lambda ts: jnp.stack([t.reshape(bc, nc_, LANES) for t in ts], axis=1).reshape(
        bc, DEPTH, nc_, KV_HEADS, HEAD_DIM)
    st5 = lambda ts: jnp.stack(ts, axis=1)
    return (y.reshape(bc, nc_, D_MODEL), z.reshape(bs, ns, D_MODEL),
            kv5(new[0]), kv5(new[1]), st5(new[2]), st5(new[3]), kv5(new[4]), kv5(new[5]))
```

```python
import functools

import jax
import jax.numpy as jnp
from jax import lax
from jax.experimental import pallas as pl
from jax.experimental.pallas import tpu as pltpu

F32 = jnp.float32
BF16 = jnp.bfloat16

D_MODEL = 1024
DEPTH = 4
GRID_W = 64
HEAD_DIM = 64
LANES = 128
GLOB_HEADS = 6
RET_HEADS = 4
WIN_HEADS = 6
KV_HEADS = 2
GROUP = GLOB_HEADS // KV_HEADS
WINDOW = 128
ROPE_BASE = 10000.0
QK_SCALE = HEAD_DIM ** -0.5
LOG2E = 1.4426950408889634
D_FF = 2816
IN_WIDTH = 2304
EPS = 1e-6
NEG = -1e30
MOD_ROWS = 8
CTX_ROW = 4

GQ0, GK0, GV0 = 0, 384, 512
RET0 = 640
WQ0, WK0, WV0 = 1664, 2048, 2176

VMEM_LIMIT = 56 * 1024 * 1024

TM_IN = 512
IN_SUBTILES = 2
TM_POST = 512
POST_SUBTILES = 2
FF_CHUNKS = ((0, 1024), (1024, 2048), (2048, D_FF))
RET_CHUNK = 256
CTX_BPB = 4
TQ_GLOB = 1024
TQ_WIN = 512
ONES_ROWS = 16


def _cparams(sem):
    return pltpu.CompilerParams(dimension_semantics=sem, vmem_limit_bytes=VMEM_LIMIT)


def _resident(block_shape, index_map):
    return pl.BlockSpec(block_shape, index_map, pipeline_mode=pl.Buffered(1))


def _dot(a, b):
    return jnp.dot(a, b, preferred_element_type=F32)


def _dot_nt(a, b):
    return lax.dot_general(a, b, (((1,), (1,)), ((), ())), preferred_element_type=F32)


def _dot_tn(a, b):
    return lax.dot_general(a, b, (((0,), (0,)), ((), ())), preferred_element_type=F32)


def _silu(x):
    return x / (1.0 + jnp.exp(-x))


def _row_rms(x):
    return x * lax.rsqrt(jnp.mean(x * x, axis=-1, keepdims=True) + EPS)


def _chunk_rows(offset, ci, chunk):
    if isinstance(ci, int):
        return pl.ds(offset + ci * chunk, chunk)
    return pl.ds(pl.multiple_of(offset + ci * chunk, chunk), chunk)


def _run_skewed(chains):
    chains = list(chains)
    done = [False] * len(chains)
    t = 0
    while not all(done):
        live = [c for c in range(min(t + 1, len(chains))) if not done[c]]
        for c in sorted(live, key=lambda c: (t - c) % 2):
            try:
                next(chains[c])
            except StopIteration:
                done[c] = True
        t += 1


def _mod_kernel(c_ref, w_ref, b_ref, o_ref):
    s = _silu(c_ref[...]).astype(BF16)
    o_ref[0] = _dot(s, w_ref[0].astype(BF16)) + b_ref[0]


def _modulation(cond8, w_mod, b_mod):
    tn = 1536
    n_out = 6 * D_MODEL
    return pl.pallas_call(
        _mod_kernel,
        grid=(DEPTH, n_out // tn),
        in_specs=[pl.BlockSpec((MOD_ROWS, D_MODEL), lambda l, j: (0, 0)),
                  pl.BlockSpec((1, D_MODEL, tn), lambda l, j: (l, 0, j)),
                  pl.BlockSpec((1, 1, tn), lambda l, j: (l, 0, j))],
        out_specs=pl.BlockSpec((1, MOD_ROWS, tn), lambda l, j: (l, 0, j)),
        out_shape=jax.ShapeDtypeStruct((DEPTH, MOD_ROWS, n_out), F32),
        compiler_params=_cparams(("parallel", "parallel")),
        name="modulation",
    )(cond8, w_mod, b_mod.reshape(DEPTH, 1, n_out))


def _mod_spec(layer, ctx, tiles_per_batch):
    if ctx:
        return pl.BlockSpec((1, 1, 6 * D_MODEL), lambda i: (layer * MOD_ROWS + CTX_ROW, 0, 0))
    return pl.BlockSpec((1, 1, 6 * D_MODEL), lambda i: (layer * MOD_ROWS + i // tiles_per_batch, 0, 0))


def _in_proj_kernel(*refs, rope, ctx):
    x_ref, mod_ref, gpre_ref, w_ref, gq_ref, gk_ref, bd_ref = refs[:7]
    pos = 7
    if rope:
        cos_ref, sa_ref, sb_ref = refs[pos:pos + 3]
        pos += 3
    qg_ref, kg_ref, vgt_ref, r_ref, qw_ref, kw_ref, vwt_ref = refs[pos:pos + 7]
    pos += 7
    if ctx:
        kg32_ref, vg32_ref, kw32_ref, vw32_ref = refs[pos:pos + 4]

    m = mod_ref[0]
    sh1 = m[:, 0:D_MODEL]
    sc1 = m[:, D_MODEL:2 * D_MODEL]
    sub = x_ref.shape[0] // IN_SUBTILES
    lane_lo = lax.broadcasted_iota(jnp.int32, (sub, LANES), 1) < HEAD_DIM

    def prologue(rows):
        return (_row_rms(x_ref[rows, :]) * (gpre_ref[...] * (1.0 + sc1)) + sh1).astype(BF16)

    def head_norm(t, g):
        ss = _dot((t * t).astype(BF16), bd_ref[...])
        return t * lax.rsqrt(ss + EPS) * g

    def rot(t, rows):
        if not rope:
            return t
        return (t * cos_ref[rows, :] + pltpu.roll(t, 16, 1) * sa_ref[rows, :]
                + pltpu.roll(t, LANES - 16, 1) * sb_ref[rows, :])

    def store_q(slabs, q_ref, rows):
        for hd in range(GLOB_HEADS):
            t = slabs[hd // 2]
            kv = hd // GROUP
            if hd % 2 != kv:
                t = pltpu.roll(t, HEAD_DIM, 1)
            keep = lane_lo if kv == 0 else jnp.logical_not(lane_lo)
            q_ref[rows, hd * LANES:(hd + 1) * LANES] = jnp.where(keep, t * (QK_SCALE * LOG2E), 0.0).astype(BF16)

    def glob_epilogue(pg, rows):
        store_q([rot(head_norm(pg[:, s * LANES:(s + 1) * LANES], gq_ref[...]), rows) for s in range(3)],
                qg_ref, rows)
        kg = head_norm(pg[:, GK0:GK0 + LANES], gk_ref[...])
        vgt = pg[:, GV0:GV0 + LANES].T
        if ctx:
            kg32_ref[:, rows] = kg.T
            vg32_ref[:, rows] = vgt
        kg_ref[rows, :] = rot(kg, rows).astype(BF16)
        vgt_ref[:, rows] = vgt.astype(BF16)

    def ret_epilogue(pr, rows):
        r_ref[rows, 0:256] = pr[:, 0:256].astype(BF16)
        r_ref[rows, 256:512] = (pr[:, 256:512] * QK_SCALE).astype(BF16)
        r_ref[rows, 512:1024] = pr[:, 512:1024].astype(BF16)

    def win_epilogue(pw, rows):
        store_q([rot(pw[:, s * LANES:(s + 1) * LANES], rows) for s in range(3)], qw_ref, rows)
        kw = pw[:, WK0 - WQ0:WK0 - WQ0 + LANES]
        vwt = pw[:, WV0 - WQ0:WV0 - WQ0 + LANES].T
        if ctx:
            kw32_ref[:, rows] = kw.T
            vw32_ref[:, rows] = vwt
        kw_ref[rows, :] = rot(kw, rows).astype(BF16)
        vwt_ref[:, rows] = vwt.astype(BF16)

    tiles = [slice(t * sub, (t + 1) * sub) for t in range(IN_SUBTILES)]
    hb = prologue(tiles[0])
    late = None
    for t, rows in enumerate(tiles):
        pg = _dot(hb, w_ref[:, GQ0:RET0])
        if late is not None:
            late()
        hb_next = prologue(tiles[t + 1]) if t + 1 < IN_SUBTILES else None
        pw = _dot(hb, w_ref[:, WQ0:IN_WIDTH])
        glob_epilogue(pg, rows)
        pr = _dot(hb, w_ref[:, RET0:WQ0])
        win_epilogue(pw, rows)
        late = functools.partial(ret_epilogue, pr, rows)
        hb = hb_next
    late()


def _in_proj(x2d, mod, gpre, w_in_b, gq_t, gk_t, bd, rope_tabs, *, layer, ctx, rows_per_batch):
    t_rows = x2d.shape[0]
    tm = TM_IN
    tiles_per_batch = rows_per_batch // tm
    rope = rope_tabs is not None
    const2 = lambda i: (0, 0)
    lay3 = lambda i: (layer, 0, 0)
    row = lambda i: (i, 0)
    col = lambda i: (0, i)
    in_specs = [pl.BlockSpec((tm, D_MODEL), row),
                _mod_spec(layer, ctx, tiles_per_batch),
                pl.BlockSpec((None, 1, D_MODEL), lay3),
                _resident((None, D_MODEL, IN_WIDTH), lay3),
                pl.BlockSpec((None, 1, LANES), lay3),
                pl.BlockSpec((None, 1, LANES), lay3),
                pl.BlockSpec((LANES, LANES), const2)]
    args = [x2d, mod, gpre, w_in_b, gq_t, gk_t, bd]
    if rope:
        tab = pl.BlockSpec((tm, LANES), lambda i: (i % tiles_per_batch, 0))
        in_specs += [tab, tab, tab]
        args += list(rope_tabs)
    bf = lambda shape: jax.ShapeDtypeStruct(shape, BF16)
    out_shape = [bf((t_rows, GLOB_HEADS * LANES)), bf((t_rows, LANES)), bf((LANES, t_rows)),
                 bf((t_rows, 4 * RET_HEADS * HEAD_DIM)),
                 bf((t_rows, WIN_HEADS * LANES)), bf((t_rows, LANES)), bf((LANES, t_rows))]
    out_specs = [pl.BlockSpec((tm, GLOB_HEADS * LANES), row), pl.BlockSpec((tm, LANES), row),
                 pl.BlockSpec((LANES, tm), col),
                 pl.BlockSpec((tm, 4 * RET_HEADS * HEAD_DIM), row),
                 pl.BlockSpec((tm, WIN_HEADS * LANES), row), pl.BlockSpec((tm, LANES), row),
                 pl.BlockSpec((LANES, tm), col)]
    if ctx:
        out_shape += [jax.ShapeDtypeStruct((LANES, t_rows), F32)] * 4
        out_specs += [pl.BlockSpec((LANES, tm), col)] * 4
    return pl.pallas_call(
        functools.partial(_in_proj_kernel, rope=rope, ctx=ctx),
        grid=(t_rows // tm,),
        in_specs=in_specs, out_specs=out_specs, out_shape=out_shape,
        compiler_params=_cparams(("parallel",)),
        name="in_proj_ctx" if ctx else "in_proj_smp",
    )(*args)


def _attn_kernel(*refs, kinds, sink_layer, sub, n_lat, stack):
    q_ref = refs[0]
    pos = 1
    srcs = []
    for kind in kinds:
        cnt = 6 if kind == "band" else 2
        srcs.append((kind, refs[pos:pos + cnt]))
        pos += cnt
    if sink_layer is not None:
        sink_ref = refs[pos]
        pos += 1
    o_ref = refs[pos]
    j = pl.program_id(1)
    nsub = q_ref.shape[0] // sub
    width = sub + 2 * WINDOW
    cols = stack * sub
    col_q = lax.broadcasted_iota(jnp.int32, (1, cols), 1)
    if "band" in kinds:
        krel = lax.broadcasted_iota(jnp.int32, (width, cols), 0) - WINDOW
        qrel = lax.broadcasted_iota(jnp.int32, (width, cols), 1) & (sub - 1)
        in_band = jnp.abs(krel - qrel) <= WINDOW

    def band_mask(bi):
        ok = in_band
        first = j * (nsub * sub) + bi * sub
        if bi == 0:
            ok = ok & (krel + first >= 0)
        if bi == nsub - 1:
            ok = ok & (krel + first < n_lat)
        return ok

    ones = lambda w: jnp.ones((ONES_ROWS, w), BF16)

    def key_rows(kind, r, bi):
        nk = r[0].shape[0]
        if kind == "split":
            return slice(bi * (nk // nsub), (bi + 1) * (nk // nsub))
        return slice(0, nk)

    band_k = {}
    band_vt = {}

    def keys(bi):
        out = []
        for kind, r in srcs:
            if kind == "band":
                if not band_k:
                    band_k[0] = jnp.concatenate([r[0][...], r[1][...], r[2][...]], axis=0)
                out.append(band_k[0][bi * sub:bi * sub + width, :])
            else:
                out.append(r[0][key_rows(kind, r, bi), :])
        return out

    def values(bi, kv):
        half = slice(HEAD_DIM * kv, HEAD_DIM * (kv + 1))
        out = []
        for kind, r in srcs:
            if kind == "band":
                if kv not in band_vt:
                    band_vt[kv] = jnp.concatenate([r[3][half, :], r[4][half, :], r[5][half, :]], axis=1)
                vt = band_vt[kv][:, bi * sub:bi * sub + width]
            else:
                vt = r[1][half, key_rows(kind, r, bi)]
            out.append(jnp.concatenate([vt, ones(vt.shape[1])], axis=0))
        return out

    def scores(bi, heads):
        rows = slice(bi * sub, (bi + 1) * sub)
        qs = [q_ref[rows, hd * LANES:(hd + 1) * LANES] for hd in heads]
        qs = qs[0] if len(qs) == 1 else jnp.concatenate(qs, axis=0)
        sts = []
        for (kind, _), k in zip(srcs, keys(bi)):
            st = _dot_nt(k, qs)
            sts.append(jnp.where(band_mask(bi), st, NEG) if kind == "band" else st)
        return sts

    def finish(bi, heads, sts):
        m = sts[0].max(axis=0, keepdims=True)
        for st in sts[1:]:
            m = jnp.maximum(m, st.max(axis=0, keepdims=True))
        if sink_layer is not None:
            sk = jnp.broadcast_to(sink_ref[sink_layer, heads[-1]] * LOG2E, (1, cols))
            for i in range(len(heads) - 2, -1, -1):
                sk = jnp.where(col_q < (i + 1) * sub, sink_ref[sink_layer, heads[i]] * LOG2E, sk)
            m = jnp.maximum(m, sk)
        ps = [jnp.exp2(st - m).astype(BF16) for st in sts]
        outs = []
        for kv in sorted({hd // GROUP for hd in heads}):
            mine = [i for i, hd in enumerate(heads) if hd // GROUP == kv]
            span = slice(mine[0] * sub, (mine[-1] + 1) * sub)
            acc = jnp.zeros((HEAD_DIM + ONES_ROWS, len(mine) * sub), F32)
            for p, vt in zip(ps, values(bi, kv)):
                acc = acc + _dot(vt, p[:, span])
            den = acc[HEAD_DIM:HEAD_DIM + 1, :]
            if sink_layer is not None:
                den = den + jnp.exp2(sk[:, span] - m[:, span])
            o = acc[:HEAD_DIM, :] / den
            outs += [o[:, i * sub:(i + 1) * sub] for i in range(len(mine))]
        return outs

    items = [(bi, list(range(g, g + stack))) for bi in range(nsub) for g in range(0, GLOB_HEADS, stack)]
    pieces = []
    pending = scores(*items[0])
    for i, (bi, heads) in enumerate(items):
        following = scores(*items[i + 1]) if i + 1 < len(items) else None
        pieces += finish(bi, heads, pending)
        pending = following

    for bi in range(nsub):
        for slab in range(GLOB_HEADS // 2):
            hd = bi * GLOB_HEADS + 2 * slab
            t = jnp.concatenate([pieces[hd], pieces[hd + 1]], axis=0)
            o_ref[bi * sub:(bi + 1) * sub, slab * LANES:(slab + 1) * LANES] = t.T.astype(BF16)


def _attention(q, sources, kinds, sink, sink_layer, *, rows, sub, stack, n, name):
    t_rows = q.shape[0]
    nq = max(n // rows, 1)
    in_specs = [pl.BlockSpec((rows, q.shape[1]), lambda b, j: (b * nq + j, 0))]
    args = [q]
    for arr, spec in sources:
        in_specs.append(spec)
        args.append(arr)
    if sink is not None:
        in_specs.append(pl.BlockSpec(memory_space=pltpu.SMEM))
        args.append(sink)
    ow = GLOB_HEADS * HEAD_DIM
    return pl.pallas_call(
        functools.partial(_attn_kernel, kinds=kinds, sink_layer=sink_layer if sink is not None else None,
                          sub=sub, n_lat=n, stack=stack),
        grid=(t_rows // (rows * nq), nq),
        in_specs=in_specs,
        out_specs=pl.BlockSpec((rows, ow), lambda b, j: (b * nq + j, 0)),
        out_shape=jax.ShapeDtypeStruct((t_rows, ow), BF16),
        compiler_params=_cparams(("parallel", "parallel")),
        name=name,
    )(*args)


def _full_source(k, vt, n):
    return [(k, pl.BlockSpec((n, LANES), lambda b, j: (b, 0))),
            (vt, pl.BlockSpec((LANES, n), lambda b, j: (0, b)))]


def _cache_source(k_all, vt_all, layer, past):
    return [(k_all, pl.BlockSpec((None, past, LANES), lambda b, j: (layer, b, 0))),
            (vt_all, pl.BlockSpec((None, LANES, past), lambda b, j: (layer, 0, b)))]


def _band_source(k, vt, n, tq):
    per = n // WINDOW
    step = tq // WINDOW
    prev = lambda b, j: b * per + jnp.maximum(step * j - 1, 0)
    nxt = lambda b, j: b * per + jnp.minimum(step * j + step, per - 1)
    mid = lambda b, j: b * (n // tq) + j
    return [(k, pl.BlockSpec((WINDOW, LANES), lambda b, j: (prev(b, j), 0))),
            (k, pl.BlockSpec((tq, LANES), lambda b, j: (mid(b, j), 0))),
            (k, pl.BlockSpec((WINDOW, LANES), lambda b, j: (nxt(b, j), 0))),
            (vt, pl.BlockSpec((LANES, WINDOW), lambda b, j: (0, prev(b, j)))),
            (vt, pl.BlockSpec((LANES, tq), lambda b, j: (0, mid(b, j)))),
            (vt, pl.BlockSpec((LANES, WINDOW), lambda b, j: (0, nxt(b, j))))]


def _ret_kernel(*refs, n, chunk, has_init, emit_state, bpb):
    r_ref, dec_ref, bd_ref = refs[:3]
    pos = 3
    if has_init:
        s0f_ref, s0b_ref = refs[pos:pos + 2]
        pos += 2
    o_ref = refs[pos]
    pos += 1
    if emit_state:
        sf_ref, sb_ref = refs[pos:pos + 2]
        pos += 2
    stf, stb, dsum, dec, cdec = refs[pos:pos + 5]
    nc = n // chunk
    n_slab = RET_HEADS // 2
    use_inter = has_init or nc > 1
    lane_lo = lax.broadcasted_iota(jnp.int32, (chunk, LANES), 1) < HEAD_DIM
    row_s = lax.broadcasted_iota(jnp.int32, (LANES, LANES), 0) < HEAD_DIM
    lane_s = lax.broadcasted_iota(jnp.int32, (LANES, LANES), 1) < HEAD_DIM
    blockdiag = row_s == lane_s

    @pl.when(pl.program_id(0) == 0)
    def _():
        x = dec_ref[...]
        lg = jnp.minimum(x, 0.0) - jnp.log(1.0 + jnp.exp(-jnp.abs(x)))
        rowc = lax.broadcasted_iota(jnp.int32, (chunk, LANES), 0).astype(F32)
        diff = (lax.broadcasted_iota(jnp.int32, (chunk, chunk), 0)
                - lax.broadcasted_iota(jnp.int32, (chunk, chunk), 1)).astype(F32)
        for hd in range(RET_HEADS):
            lf = lg[hd:hd + 1, 0:1]
            lb = lg[RET_HEADS + hd:RET_HEADS + hd + 1, 0:1]
            dsum[hd] = (jnp.where(diff >= 0, jnp.exp(jnp.maximum(diff, 0.0) * lf), 0.0)
                        + jnp.where(diff <= 0, jnp.exp(jnp.maximum(-diff, 0.0) * lb), 0.0))
        for sl in range(n_slab):
            f0, f1 = lg[2 * sl:2 * sl + 1], lg[2 * sl + 1:2 * sl + 2]
            b0 = lg[RET_HEADS + 2 * sl:RET_HEADS + 2 * sl + 1]
            b1 = lg[RET_HEADS + 2 * sl + 1:RET_HEADS + 2 * sl + 2]
            lf2 = jnp.where(lane_lo, f0, f1)
            lb2 = jnp.where(lane_lo, b0, b1)
            dec[0 + sl] = jnp.exp((chunk - 1.0 - rowc) * lf2)
            dec[2 + sl] = jnp.exp(rowc * lb2)
            dec[4 + sl] = jnp.exp((rowc + 1.0) * lf2)
            dec[6 + sl] = jnp.exp((chunk - rowc) * lb2)
            cdec[sl] = jnp.exp(chunk * jnp.where(row_s, f0, f1))
            cdec[n_slab + sl] = jnp.exp(chunk * jnp.where(row_s, b0, b1))

    lo, hi = slice(0, HEAD_DIM), slice(HEAD_DIM, LANES)
    for bi in range(bpb):
        for sl in range(n_slab):
            stf[bi, 0, sl] = jnp.zeros((LANES, LANES), F32)
            stb[bi, nc, sl] = jnp.zeros((LANES, LANES), F32)
            if has_init:
                stf[bi, 0, sl, lo, lo] = s0f_ref[bi, 2 * sl]
                stf[bi, 0, sl, hi, hi] = s0f_ref[bi, 2 * sl + 1]
                stb[bi, nc, sl, lo, lo] = s0b_ref[bi, 2 * sl]
                stb[bi, nc, sl, hi, hi] = s0b_ref[bi, 2 * sl + 1]

    def state_chain(bi, ci, sl):
        rows = _chunk_rows(bi * n, ci, chunk)
        yield
        k2 = r_ref[rows, 256 + sl * LANES:256 + (sl + 1) * LANES].astype(F32)
        kk = jnp.concatenate([(k2 * dec[0 + sl]).astype(BF16), (k2 * dec[2 + sl]).astype(BF16)], axis=1)
        yield
        u = _dot_tn(kk, r_ref[rows, 512 + sl * LANES:512 + (sl + 1) * LANES])
        yield
        stf[bi, ci + 1, sl] = jnp.where(blockdiag, u[:LANES], 0.0)
        stb[bi, ci, sl] = jnp.where(blockdiag, u[LANES:], 0.0)

    def output_chain(bi, ci, sl):
        rows = _chunk_rows(bi * n, ci, chunk)
        cols = lambda off: slice(off + sl * LANES, off + (sl + 1) * LANES)
        q2 = r_ref[rows, cols(0)]
        k2 = r_ref[rows, cols(256)]
        zero = jnp.zeros_like(q2)
        s_lo = _dot_nt(jnp.where(lane_lo, q2, zero), k2)
        s_hi = _dot_nt(jnp.where(lane_lo, zero, q2), k2)
        if use_inter:
            it = _dot(q2, jnp.concatenate([stf[bi, ci, sl], stb[bi, ci + 1, sl]], axis=1).astype(BF16))
        yield
        a_lo = (s_lo * dsum[2 * sl]).astype(BF16)
        a_hi = (s_hi * dsum[2 * sl + 1]).astype(BF16)
        yield
        v2 = r_ref[rows, cols(512)]
        o_lo, o_hi = _dot(a_lo, v2), _dot(a_hi, v2)
        yield
        o2 = jnp.where(lane_lo, o_lo, o_hi)
        if use_inter:
            o2 = o2 + it[:, :LANES] * dec[4 + sl] + it[:, LANES:] * dec[6 + sl]
        sq = (o2 * o2).astype(BF16)
        yield
        ss = _dot(sq, bd_ref[...])
        yield
        g2 = r_ref[rows, cols(768)].astype(F32)
        o_ref[rows, sl * LANES:(sl + 1) * LANES] = (o2 * lax.rsqrt(ss + EPS) * _silu(g2)).astype(BF16)

    def scan_states(bi):
        for sl in range(n_slab):
            for ci in range(nc):
                stf[bi, ci + 1, sl] = stf[bi, ci, sl] * cdec[sl] + stf[bi, ci + 1, sl]
            for ci in range(nc - 1, -1, -1):
                stb[bi, ci, sl] = stb[bi, ci + 1, sl] * cdec[n_slab + sl] + stb[bi, ci, sl]
            if emit_state:
                sf_ref[bi, 2 * sl] = stf[bi, nc, sl, lo, lo]
                sf_ref[bi, 2 * sl + 1] = stf[bi, nc, sl, hi, hi]
                sb_ref[bi, 2 * sl] = stb[bi, 0, sl, lo, lo]
                sb_ref[bi, 2 * sl + 1] = stb[bi, 0, sl, hi, hi]

    slabs = range(n_slab)
    if not use_inter:
        _run_skewed([state_chain(bi, 0, sl) for bi in range(bpb) for sl in slabs]
                    + [output_chain(bi, 0, sl) for bi in range(bpb) for sl in slabs])
        for bi in range(bpb):
            scan_states(bi)
    else:
        for bi in range(bpb):
            def local_state(ci, carry):
                _run_skewed([state_chain(bi, ci, sl) for sl in slabs])
                return carry

            def outputs(ci, carry):
                _run_skewed([output_chain(bi, ci, sl) for sl in slabs])
                return carry

            lax.fori_loop(0, nc, local_state, 0)
            scan_states(bi)
            lax.fori_loop(0, nc, outputs, 0)


def _retention(r, dec_tab, bd, init, *, layer, bsz, n, emit_state, name, bpb=1):
    rw = r.shape[1]
    chunk = RET_CHUNK
    nc = n // chunk
    n_slab = RET_HEADS // 2
    has_init = init is not None
    in_specs = [pl.BlockSpec((bpb * n, rw), lambda b: (b, 0)),
                pl.BlockSpec((None, 8, LANES), lambda b: (layer, 0, 0)),
                pl.BlockSpec((LANES, LANES), lambda b: (0, 0))]
    args = [r, dec_tab, bd]
    if has_init:
        init_spec = pl.BlockSpec((bpb, None, RET_HEADS, HEAD_DIM, HEAD_DIM), lambda b: (b, layer, 0, 0, 0))
        in_specs += [init_spec, init_spec]
        args += list(init)
    ow = RET_HEADS * HEAD_DIM
    out_shape = [jax.ShapeDtypeStruct((bsz * n, ow), BF16)]
    out_specs = [pl.BlockSpec((bpb * n, ow), lambda b: (b, 0))]
    if emit_state:
        st_spec = pl.BlockSpec((bpb, RET_HEADS, HEAD_DIM, HEAD_DIM), lambda b: (b, 0, 0, 0))
        out_shape += [jax.ShapeDtypeStruct((bsz, RET_HEADS, HEAD_DIM, HEAD_DIM), F32)] * 2
        out_specs += [st_spec, st_spec]
    return pl.pallas_call(
        functools.partial(_ret_kernel, n=n, chunk=chunk, has_init=has_init, emit_state=emit_state, bpb=bpb),
        grid=(bsz // bpb,),
        in_specs=in_specs, out_specs=out_specs, out_shape=out_shape,
        scratch_shapes=[pltpu.VMEM((bpb, nc + 1, n_slab, LANES, LANES), F32),
                        pltpu.VMEM((bpb, nc + 1, n_slab, LANES, LANES), F32),
                        pltpu.VMEM((RET_HEADS, chunk, chunk), F32),
                        pltpu.VMEM((8, chunk, LANES), F32),
                        pltpu.VMEM((2 * n_slab, LANES, LANES), F32)],
        compiler_params=_cparams(("arbitrary",)),
        name=name,
    )(*args)


def _post_kernel(x_ref, og_ref, or_ref, ow_ref, mod_ref, wo_ref, wgu_ref, wd_ref,
                 gpost_ref, gpre2_ref, gpost2_ref, o_ref):
    m = mod_ref[0]
    gt1 = m[:, 2 * D_MODEL:3 * D_MODEL]
    sh2 = m[:, 3 * D_MODEL:4 * D_MODEL]
    sc2 = m[:, 4 * D_MODEL:5 * D_MODEL]
    gt2 = m[:, 5 * D_MODEL:6 * D_MODEL]
    sub = x_ref.shape[0] // POST_SUBTILES
    tiles = [slice(t * sub, (t + 1) * sub) for t in range(POST_SUBTILES)]

    def out_proj(rows):
        merged = jnp.concatenate([og_ref[rows, :], or_ref[rows, :], ow_ref[rows, :]], axis=1)
        return _dot(merged, wo_ref[...])

    def norms(rows, mixed):
        x1 = x_ref[rows, :] + gt1 * (_row_rms(mixed) * gpost_ref[...])
        return x1, (_row_rms(x1) * (gpre2_ref[...] * (1.0 + sc2)) + sh2).astype(BF16)

    def ffn(h2):
        ff = None
        for lo, hi in FF_CHUNKS:
            act = (_silu(_dot(h2, wgu_ref[:, lo:hi])) * _dot(h2, wgu_ref[:, D_FF + lo:D_FF + hi])).astype(BF16)
            part = _dot(act, wd_ref[lo:hi, :])
            ff = part if ff is None else ff + part
        return ff

    mixed = [out_proj(rows) for rows in tiles]
    x1s, ffs = [], []
    for rows, mx in zip(tiles, mixed):
        x1, h2 = norms(rows, mx)
        x1s.append(x1)
        ffs.append(ffn(h2))
    for rows, x1, ff in zip(tiles, x1s, ffs):
        o_ref[rows, :] = x1 + gt2 * (_row_rms(ff) * gpost2_ref[...])


def _post(x2d, og, orr, ow, mod, w_out_b, w_gu_b, w_down_b, gpost, gpre2, gpost2, *, layer, ctx, rows_per_batch):
    t_rows = x2d.shape[0]
    tm = TM_POST
    row = lambda i: (i, 0)
    lay3 = lambda i: (layer, 0, 0)
    gain = pl.BlockSpec((None, 1, D_MODEL), lay3)
    return pl.pallas_call(
        _post_kernel,
        grid=(t_rows // tm,),
        in_specs=[pl.BlockSpec((tm, D_MODEL), row),
                  pl.BlockSpec((tm, og.shape[1]), row),
                  pl.BlockSpec((tm, orr.shape[1]), row),
                  pl.BlockSpec((tm, ow.shape[1]), row),
                  _mod_spec(layer, ctx, rows_per_batch // tm),
                  _resident((None, D_MODEL, D_MODEL), lay3),
                  _resident((None, D_MODEL, 2 * D_FF), lay3),
                  _resident((None, D_FF, D_MODEL), lay3),
                  gain, gain, gain],
        out_specs=pl.BlockSpec((tm, D_MODEL), row),
        out_shape=jax.ShapeDtypeStruct((t_rows, D_MODEL), F32),
        compiler_params=_cparams(("parallel",)),
        name="post_mixer",
    )(x2d, og, orr, ow, mod, w_out_b, w_gu_b, w_down_b, gpost, gpre2, gpost2)


def _rope_tables(n):
    half = HEAD_DIM // 4
    pos = jnp.arange(n, dtype=jnp.int32)
    row = (pos // GRID_W).astype(F32)
    col = (pos % GRID_W).astype(F32)
    freqs = ROPE_BASE ** (-jnp.arange(half, dtype=F32) / half)
    ang_r = row[:, None] * freqs[None, :]
    ang_c = col[:, None] * freqs[None, :]
    ang = jnp.concatenate([ang_r, ang_r, ang_c, ang_c], axis=-1)
    second = (jnp.arange(HEAD_DIM) % (2 * half)) >= half
    cos = jnp.cos(ang)
    sin = jnp.sin(ang)
    sin_a = jnp.where(second[None, :], sin, 0.0)
    sin_b = jnp.where(second[None, :], 0.0, -sin)
    tile = lambda t: jnp.concatenate([t, t], axis=-1)
    return tile(cos), tile(sin_a), tile(sin_b)


def _cache_layouts(ck, cv):
    b, _, past = ck.shape[:3]
    k = jnp.transpose(ck.reshape(b, DEPTH, past, LANES), (1, 0, 2, 3)).reshape(DEPTH, b * past, LANES)
    vt = jnp.transpose(cv.reshape(b, DEPTH, past, LANES), (1, 3, 0, 2)).reshape(DEPTH, LANES, b * past)
    return k.astype(BF16), vt.astype(BF16)


def kernel(x_prompt, x_sample, cache_glob_k, cache_glob_v, state_ret_fwd, state_ret_bwd, cache_win_k, cache_win_v, c, c_ctx, w_mod, b_mod, g_pre_mix, g_post_mix, g_pre_ffn, g_post_ffn, w_in, g_q, g_k, ret_decay_fwd, ret_decay_bwd, win_sink, w_out, w_gate_up, w_down):
    bc, nc_, _ = x_prompt.shape
    bs, ns, _ = x_sample.shape
    past = cache_glob_k.shape[2]

    cond8 = jnp.concatenate([c, c_ctx[None, :], jnp.zeros((MOD_ROWS - bs - 1, D_MODEL), F32)], axis=0)
    mod = _modulation(cond8, w_mod, b_mod).reshape(DEPTH * MOD_ROWS, 1, 6 * D_MODEL)

    w_in_b = w_in.astype(BF16)
    w_out_b = w_out.astype(BF16)
    w_gu_b = w_gate_up.astype(BF16)
    w_down_b = w_down.astype(BF16)
    rope_tabs = _rope_tables(ns)
    seg = jnp.arange(LANES) // HEAD_DIM
    bd = jnp.where(seg[:, None] == seg[None, :], 1.0 / HEAD_DIM, 0.0).astype(BF16)
    gain3 = lambda g: g.reshape(DEPTH, 1, D_MODEL)
    gpre, gpost, gpre2, gpost2 = gain3(g_pre_mix), gain3(g_post_mix), gain3(g_pre_ffn), gain3(g_post_ffn)
    gq_t = jnp.tile(g_q, (1, 2)).reshape(DEPTH, 1, LANES)
    gk_t = jnp.tile(g_k, (1, 2)).reshape(DEPTH, 1, LANES)
    dec_tab = jnp.broadcast_to(jnp.concatenate([ret_decay_fwd, ret_decay_bwd], axis=1)[:, :, None],
                               (DEPTH, 2 * RET_HEADS, LANES))
    init = (state_ret_fwd, state_ret_bwd)
    cgk, cgvt = _cache_layouts(cache_glob_k, cache_glob_v)
    cwk, cwvt = _cache_layouts(cache_win_k, cache_win_v)

    y = x_prompt.reshape(bc * nc_, D_MODEL)
    z = x_sample.reshape(bs * ns, D_MODEL)
    new = [[] for _ in range(6)]
    for l in range(DEPTH):
        qg, kg, vgt, r, qw, kw, vwt, kg32, vg32, kw32, vw32 = _in_proj(
            y, mod, gpre, w_in_b, gq_t, gk_t, bd, None, layer=l, ctx=True, rows_per_batch=nc_)
        og = _attention(qg, _full_source(kg, vgt, nc_ * CTX_BPB), ("split",), None, l,
                        rows=nc_ * CTX_BPB, sub=nc_, stack=GROUP, n=nc_, name="attn_glob_ctx")
        ow = _attention(qw, _full_source(kw, vwt, nc_ * CTX_BPB), ("split",), win_sink, l,
                        rows=nc_ * CTX_BPB, sub=nc_, stack=GROUP, n=nc_, name="attn_win_ctx")
        orr, sf, sb = _retention(r, dec_tab, bd, None, layer=l, bsz=bc, n=nc_, emit_state=True, bpb=CTX_BPB,
                                 name="ret_ctx")
        y = _post(y, og, orr, ow, mod, w_out_b, w_gu_b, w_down_b, gpost, gpre2, gpost2,
                  layer=l, ctx=True, rows_per_batch=nc_)
        for lst, t in zip(new, (kg32, vg32, sf, sb, kw32, vw32)):
            lst.append(t)

        qg, kg, vgt, r, qw, kw, vwt = _in_proj(
            z, mod, gpre, w_in_b, gq_t, gk_t, bd, rope_tabs, layer=l, ctx=False, rows_per_batch=ns)
        og = _attention(qg, _full_source(kg, vgt, ns) + _cache_source(cgk, cgvt, l, past),
                        ("shared", "shared"), None, l,
                        rows=TQ_GLOB, sub=TQ_GLOB, stack=1, n=ns, name="attn_glob_smp")
        ow = _attention(qw, _band_source(kw, vwt, ns, TQ_WIN) + _cache_source(cwk, cwvt, l, past),
                        ("band", "shared"), win_sink, l,
                        rows=TQ_WIN, sub=WINDOW, stack=GLOB_HEADS, n=ns, name="attn_win_smp")
        (orr,) = _retention(r, dec_tab, bd, init, layer=l, bsz=bs, n=ns, emit_state=False, name="ret_smp")
        z = _post(z, og, orr, ow, mod, w_out_b, w_gu_b, w_down_b, gpost, gpre2, gpost2,
                  layer=l, ctx=False, rows_per_batch=ns)

    kv5 = lambda ts: jnp.transpose(jnp.stack(ts, axis=0).reshape(DEPTH, KV_HEADS, HEAD_DIM, bc, nc_),
                                   (3, 0, 4, 1, 2))
    st5 = lambda ts: jnp.stack(ts, axis=1)
    return (y.reshape(bc, nc_, D_MODEL), z.reshape(bs, ns, D_MODEL),
            kv5(new[0]), kv5(new[1]), st5(new[2]), st5(new[3]), kv5(new[4]), kv5(new[5]))
```

```python
import functools

import jax
import jax.numpy as jnp
from jax import lax
from jax.experimental import pallas as pl
from jax.experimental.pallas import tpu as pltpu

F32 = jnp.float32
BF16 = jnp.bfloat16

D_MODEL = 1024
DEPTH = 4
GRID_W = 64
HEAD_DIM = 64
LANES = 128
GLOB_HEADS = 6
RET_HEADS = 4
WIN_HEADS = 6
KV_HEADS = 2
GROUP = GLOB_HEADS // KV_HEADS
WINDOW = 128
ROPE_BASE = 10000.0
QK_SCALE = HEAD_DIM ** -0.5
LOG2E = 1.4426950408889634
D_FF = 2816
IN_WIDTH = 2304
EPS = 1e-6
NEG = -1e30
MOD_ROWS = 8
CTX_ROW = 4

GQ0, GK0, GV0 = 0, 384, 512
RET0 = 640
WQ0, WK0, WV0 = 1664, 2048, 2176

VMEM_LIMIT = 56 * 1024 * 1024

TM_IN = 1024
IN_SUBTILES = 4
TM_POST = 512
POST_SUBTILES = 2
FF_CHUNKS = ((0, 1024), (1024, 2048), (2048, D_FF))
RET_CHUNK = 256
RET_CHUNK_GROUP = 2
CTX_BPB = 4
TQ_GLOB = 1024
TQ_WIN = 1024
ONES_ROWS = 16


def _cparams(sem):
    return pltpu.CompilerParams(dimension_semantics=sem, vmem_limit_bytes=VMEM_LIMIT)


def _resident(block_shape, index_map):
    return pl.BlockSpec(block_shape, index_map, pipeline_mode=pl.Buffered(1))


def _dot(a, b):
    return jnp.dot(a, b, preferred_element_type=F32)


def _dot_nt(a, b):
    return lax.dot_general(a, b, (((1,), (1,)), ((), ())), preferred_element_type=F32)


def _dot_tn(a, b):
    return lax.dot_general(a, b, (((0,), (0,)), ((), ())), preferred_element_type=F32)


def _silu(x):
    return x / (1.0 + jnp.exp(-x))


def _row_rms(x):
    return x * lax.rsqrt(jnp.mean(x * x, axis=-1, keepdims=True) + EPS)


def _chunk_rows(offset, ci, chunk):
    if isinstance(ci, int):
        return pl.ds(offset + ci * chunk, chunk)
    return pl.ds(pl.multiple_of(offset + ci * chunk, chunk), chunk)


def _run_skewed(chains):
    chains = list(chains)
    done = [False] * len(chains)
    t = 0
    while not all(done):
        live = [c for c in range(min(t + 1, len(chains))) if not done[c]]
        for c in sorted(live, key=lambda c: (t - c) % 2):
            try:
                next(chains[c])
            except StopIteration:
                done[c] = True
        t += 1


def _mod_kernel(c_ref, w_ref, b_ref, o_ref):
    s = _silu(c_ref[...]).astype(BF16)
    o_ref[0] = _dot(s, w_ref[0].astype(BF16)) + b_ref[0]


def _modulation(cond8, w_mod, b_mod):
    tn = 1536
    n_out = 6 * D_MODEL
    return pl.pallas_call(
        _mod_kernel,
        grid=(DEPTH, n_out // tn),
        in_specs=[pl.BlockSpec((MOD_ROWS, D_MODEL), lambda l, j: (0, 0)),
                  pl.BlockSpec((1, D_MODEL, tn), lambda l, j: (l, 0, j)),
                  pl.BlockSpec((1, 1, tn), lambda l, j: (l, 0, j))],
        out_specs=pl.BlockSpec((1, MOD_ROWS, tn), lambda l, j: (l, 0, j)),
        out_shape=jax.ShapeDtypeStruct((DEPTH, MOD_ROWS, n_out), F32),
        compiler_params=_cparams(("parallel", "parallel")),
        name="modulation",
    )(cond8, w_mod, b_mod.reshape(DEPTH, 1, n_out))


def _mod_spec(layer, ctx, tiles_per_batch):
    if ctx:
        return pl.BlockSpec((1, 1, 6 * D_MODEL), lambda i: (layer * MOD_ROWS + CTX_ROW, 0, 0))
    return pl.BlockSpec((1, 1, 6 * D_MODEL), lambda i: (layer * MOD_ROWS + i // tiles_per_batch, 0, 0))


def _in_proj_kernel(*refs, rope, ctx):
    x_ref, mod_ref, gpre_ref, w_ref, gq_ref, gk_ref, bd_ref = refs[:7]
    pos = 7
    if rope:
        cos_ref, sa_ref, sb_ref = refs[pos:pos + 3]
        pos += 3
    qg_ref, kg_ref, vgt_ref, r_ref, qw_ref, kw_ref, vwt_ref = refs[pos:pos + 7]
    pos += 7
    if ctx:
        kg32_ref, vg32_ref, kw32_ref, vw32_ref = refs[pos:pos + 4]

    m = mod_ref[0]
    sh1 = m[:, 0:D_MODEL]
    sc1 = m[:, D_MODEL:2 * D_MODEL]
    sub = x_ref.shape[0] // IN_SUBTILES
    lane_lo = lax.broadcasted_iota(jnp.int32, (sub, LANES), 1) < HEAD_DIM

    def prologue(rows):
        return (_row_rms(x_ref[rows, :]) * (gpre_ref[...] * (1.0 + sc1)) + sh1).astype(BF16)

    def head_norm(t, g):
        ss = _dot((t * t).astype(BF16), bd_ref[...])
        return t * lax.rsqrt(ss + EPS) * g

    def rot(t, rows):
        if not rope:
            return t
        return (t * cos_ref[rows, :] + pltpu.roll(t, 16, 1) * sa_ref[rows, :]
                + pltpu.roll(t, LANES - 16, 1) * sb_ref[rows, :])

    def store_q(slabs, q_ref, rows):
        for hd in range(GLOB_HEADS):
            t = slabs[hd // 2]
            kv = hd // GROUP
            if hd % 2 != kv:
                t = pltpu.roll(t, HEAD_DIM, 1)
            keep = lane_lo if kv == 0 else jnp.logical_not(lane_lo)
            q_ref[rows, hd * LANES:(hd + 1) * LANES] = jnp.where(keep, t * (QK_SCALE * LOG2E), 0.0).astype(BF16)

    def glob_epilogue(pg, rows):
        store_q([rot(head_norm(pg[:, s * LANES:(s + 1) * LANES], gq_ref[...]), rows) for s in range(3)],
                qg_ref, rows)
        kg = head_norm(pg[:, GK0:GK0 + LANES], gk_ref[...])
        vgt = pg[:, GV0:GV0 + LANES].T
        if ctx:
            kg32_ref[rows.start // sub] = kg.T
            vg32_ref[rows.start // sub] = vgt
        kg_ref[rows, :] = rot(kg, rows).astype(BF16)
        vgt_ref[:, rows] = vgt.astype(BF16)

    def ret_store(p, rows, lo, hi, scale=None):
        r_ref[rows, lo:hi] = (p if scale is None else p * scale).astype(BF16)

    def win_epilogue(pw, rows):
        store_q([rot(pw[:, s * LANES:(s + 1) * LANES], rows) for s in range(3)], qw_ref, rows)
        kw = pw[:, WK0 - WQ0:WK0 - WQ0 + LANES]
        vwt = pw[:, WV0 - WQ0:WV0 - WQ0 + LANES].T
        if ctx:
            kw32_ref[rows.start // sub] = kw.T
            vw32_ref[rows.start // sub] = vwt
        kw_ref[rows, :] = rot(kw, rows).astype(BF16)
        vwt_ref[:, rows] = vwt.astype(BF16)

    third = IN_WIDTH // 3
    rel = lambda c0: c0 - RET0

    def a_epilogue(pa, rows):
        glob_epilogue(pa[:, :RET0], rows)
        ret_store(pa[:, RET0:], rows, 0, rel(third))

    def b_epilogue(pb, rows):
        ret_store(pb[:, :LANES], rows, rel(third), 256)
        ret_store(pb[:, LANES:LANES + 256], rows, 256, 512, QK_SCALE)
        ret_store(pb[:, LANES + 256:], rows, 512, rel(2 * third))

    def c_epilogue(pc, rows):
        ret_store(pc[:, :WQ0 - 2 * third], rows, rel(2 * third), 1024)
        win_epilogue(pc[:, WQ0 - 2 * third:], rows)

    tiles = [slice(t * sub, (t + 1) * sub) for t in range(IN_SUBTILES)]
    hb = prologue(tiles[0])
    late = None
    for t, rows in enumerate(tiles):
        pa = _dot(hb, w_ref[:, 0:third])
        if late is not None:
            late()
        hb_next = prologue(tiles[t + 1]) if t + 1 < IN_SUBTILES else None
        pc = _dot(hb, w_ref[:, 2 * third:IN_WIDTH])
        a_epilogue(pa, rows)
        pb = _dot(hb, w_ref[:, third:2 * third])
        c_epilogue(pc, rows)
        late = functools.partial(b_epilogue, pb, rows)
        hb = hb_next
    late()


def _in_proj(x2d, mod, gpre, w_in_b, gq_t, gk_t, bd, rope_tabs, *, layer, ctx, rows_per_batch):
    t_rows = x2d.shape[0]
    tm = TM_IN
    tiles_per_batch = rows_per_batch // tm
    rope = rope_tabs is not None
    const2 = lambda i: (0, 0)
    lay3 = lambda i: (layer, 0, 0)
    row = lambda i: (i, 0)
    col = lambda i: (0, i)
    in_specs = [pl.BlockSpec((tm, D_MODEL), row),
                _mod_spec(layer, ctx, tiles_per_batch),
                pl.BlockSpec((None, 1, D_MODEL), lay3),
                _resident((None, D_MODEL, IN_WIDTH), lay3),
                pl.BlockSpec((None, 1, LANES), lay3),
                pl.BlockSpec((None, 1, LANES), lay3),
                pl.BlockSpec((LANES, LANES), const2)]
    args = [x2d, mod, gpre, w_in_b, gq_t, gk_t, bd]
    if rope:
        tab = pl.BlockSpec((tm, LANES), lambda i: (i % tiles_per_batch, 0))
        in_specs += [tab, tab, tab]
        args += list(rope_tabs)
    bf = lambda shape: jax.ShapeDtypeStruct(shape, BF16)
    out_shape = [bf((t_rows, GLOB_HEADS * LANES)), bf((t_rows, LANES)), bf((LANES, t_rows)),
                 bf((t_rows, 4 * RET_HEADS * HEAD_DIM)),
                 bf((t_rows, WIN_HEADS * LANES)), bf((t_rows, LANES)), bf((LANES, t_rows))]
    out_specs = [pl.BlockSpec((tm, GLOB_HEADS * LANES), row), pl.BlockSpec((tm, LANES), row),
                 pl.BlockSpec((LANES, tm), col),
                 pl.BlockSpec((tm, 4 * RET_HEADS * HEAD_DIM), row),
                 pl.BlockSpec((tm, WIN_HEADS * LANES), row), pl.BlockSpec((tm, LANES), row),
                 pl.BlockSpec((LANES, tm), col)]
    if ctx:
        assert tm // IN_SUBTILES == rows_per_batch
        out_shape += [jax.ShapeDtypeStruct((t_rows // rows_per_batch, LANES, rows_per_batch), F32)] * 4
        out_specs += [pl.BlockSpec((IN_SUBTILES, LANES, rows_per_batch), lambda i: (i, 0, 0))] * 4
    return pl.pallas_call(
        functools.partial(_in_proj_kernel, rope=rope, ctx=ctx),
        grid=(t_rows // tm,),
        in_specs=in_specs, out_specs=out_specs, out_shape=out_shape,
        compiler_params=_cparams(("parallel",)),
        name="in_proj_ctx" if ctx else "in_proj_smp",
    )(*args)


def _attn_kernel(*refs, kinds, sink_layer, sub, n_lat, stack):
    q_ref = refs[0]
    pos = 1
    srcs = []
    for kind in kinds:
        cnt = 6 if kind == "band" else 2
        srcs.append((kind, refs[pos:pos + cnt]))
        pos += cnt
    if sink_layer is not None:
        sink_ref = refs[pos]
        pos += 1
    o_ref = refs[pos]
    j = pl.program_id(1)
    nsub = q_ref.shape[0] // sub
    width = sub + 2 * WINDOW
    cols = stack * sub
    col_q = lax.broadcasted_iota(jnp.int32, (1, cols), 1)
    if "band" in kinds:
        krel = lax.broadcasted_iota(jnp.int32, (width, cols), 0) - WINDOW
        qrel = lax.broadcasted_iota(jnp.int32, (width, cols), 1) & (sub - 1)
        in_band = jnp.abs(krel - qrel) <= WINDOW

    def band_mask(bi):
        ok = in_band
        first = j * (nsub * sub) + bi * sub
        if bi == 0:
            ok = ok & (krel + first >= 0)
        if bi == nsub - 1:
            ok = ok & (krel + first < n_lat)
        return ok

    ones = lambda w: jnp.ones((ONES_ROWS, w), BF16)

    def key_rows(kind, r, bi):
        nk = r[0].shape[0]
        if kind == "split":
            return slice(bi * (nk // nsub), (bi + 1) * (nk // nsub))
        return slice(0, nk)

    band_k = {}
    band_vt = {}

    def keys(bi):
        out = []
        for kind, r in srcs:
            if kind == "band":
                if not band_k:
                    band_k[0] = jnp.concatenate([r[0][...], r[1][...], r[2][...]], axis=0)
                out.append(band_k[0][bi * sub:bi * sub + width, :])
            else:
                out.append(r[0][key_rows(kind, r, bi), :])
        return out

    def values(bi, kv):
        half = slice(HEAD_DIM * kv, HEAD_DIM * (kv + 1))
        out = []
        for kind, r in srcs:
            if kind == "band":
                if kv not in band_vt:
                    band_vt[kv] = jnp.concatenate([r[3][half, :], r[4][half, :], r[5][half, :]], axis=1)
                vt = band_vt[kv][:, bi * sub:bi * sub + width]
            else:
                vt = r[1][half, key_rows(kind, r, bi)]
            out.append(jnp.concatenate([vt, ones(vt.shape[1])], axis=0))
        return out

    def scores(bi, heads):
        rows = slice(bi * sub, (bi + 1) * sub)
        qs = [q_ref[rows, hd * LANES:(hd + 1) * LANES] for hd in heads]
        qs = qs[0] if len(qs) == 1 else jnp.concatenate(qs, axis=0)
        sts = []
        for (kind, _), k in zip(srcs, keys(bi)):
            st = _dot_nt(k, qs)
            sts.append(jnp.where(band_mask(bi), st, NEG) if kind == "band" else st)
        return sts

    def finish(bi, heads, sts):
        m = sts[0].max(axis=0, keepdims=True)
        for st in sts[1:]:
            m = jnp.maximum(m, st.max(axis=0, keepdims=True))
        if sink_layer is not None:
            sk = jnp.broadcast_to(sink_ref[sink_layer, heads[-1]] * LOG2E, (1, cols))
            for i in range(len(heads) - 2, -1, -1):
                sk = jnp.where(col_q < (i + 1) * sub, sink_ref[sink_layer, heads[i]] * LOG2E, sk)
            m = jnp.maximum(m, sk)
        ps = [jnp.exp2(st - m).astype(BF16) for st in sts]
        outs = []
        for kv in sorted({hd // GROUP for hd in heads}):
            mine = [i for i, hd in enumerate(heads) if hd // GROUP == kv]
            span = slice(mine[0] * sub, (mine[-1] + 1) * sub)
            acc = jnp.zeros((HEAD_DIM + ONES_ROWS, len(mine) * sub), F32)
            for p, vt in zip(ps, values(bi, kv)):
                acc = acc + _dot(vt, p[:, span])
            den = acc[HEAD_DIM:HEAD_DIM + 1, :]
            if sink_layer is not None:
                den = den + jnp.exp2(sk[:, span] - m[:, span])
            o = acc[:HEAD_DIM, :] / den
            outs += [o[:, i * sub:(i + 1) * sub] for i in range(len(mine))]
        return outs

    items = [(bi, list(range(g, g + stack))) for bi in range(nsub) for g in range(0, GLOB_HEADS, stack)]
    pieces = []
    pending = scores(*items[0])
    for i, (bi, heads) in enumerate(items):
        following = scores(*items[i + 1]) if i + 1 < len(items) else None
        pieces += finish(bi, heads, pending)
        pending = following

    for bi in range(nsub):
        for slab in range(GLOB_HEADS // 2):
            hd = bi * GLOB_HEADS + 2 * slab
            t = jnp.concatenate([pieces[hd], pieces[hd + 1]], axis=0)
            o_ref[bi * sub:(bi + 1) * sub, slab * LANES:(slab + 1) * LANES] = t.T.astype(BF16)


def _attention(q, sources, kinds, sink, sink_layer, *, rows, sub, stack, n, name):
    t_rows = q.shape[0]
    nq = max(n // rows, 1)
    in_specs = [pl.BlockSpec((rows, q.shape[1]), lambda b, j: (b * nq + j, 0))]
    args = [q]
    for arr, spec in sources:
        in_specs.append(spec)
        args.append(arr)
    if sink is not None:
        in_specs.append(pl.BlockSpec(memory_space=pltpu.SMEM))
        args.append(sink)
    ow = GLOB_HEADS * HEAD_DIM
    return pl.pallas_call(
        functools.partial(_attn_kernel, kinds=kinds, sink_layer=sink_layer if sink is not None else None,
                          sub=sub, n_lat=n, stack=stack),
        grid=(t_rows // (rows * nq), nq),
        in_specs=in_specs,
        out_specs=pl.BlockSpec((rows, ow), lambda b, j: (b * nq + j, 0)),
        out_shape=jax.ShapeDtypeStruct((t_rows, ow), BF16),
        compiler_params=_cparams(("parallel", "parallel")),
        name=name,
    )(*args)


def _full_source(k, vt, n):
    return [(k, pl.BlockSpec((n, LANES), lambda b, j: (b, 0))),
            (vt, pl.BlockSpec((LANES, n), lambda b, j: (0, b)))]


def _cache_source(k_all, vt_all, layer, past):
    return [(k_all, pl.BlockSpec((None, past, LANES), lambda b, j: (layer, b, 0))),
            (vt_all, pl.BlockSpec((None, LANES, past), lambda b, j: (layer, 0, b)))]


def _band_source(k, vt, n, tq):
    per = n // WINDOW
    step = tq // WINDOW
    prev = lambda b, j: b * per + jnp.maximum(step * j - 1, 0)
    nxt = lambda b, j: b * per + jnp.minimum(step * j + step, per - 1)
    mid = lambda b, j: b * (n // tq) + j
    return [(k, pl.BlockSpec((WINDOW, LANES), lambda b, j: (prev(b, j), 0))),
            (k, pl.BlockSpec((tq, LANES), lambda b, j: (mid(b, j), 0))),
            (k, pl.BlockSpec((WINDOW, LANES), lambda b, j: (nxt(b, j), 0))),
            (vt, pl.BlockSpec((LANES, WINDOW), lambda b, j: (0, prev(b, j)))),
            (vt, pl.BlockSpec((LANES, tq), lambda b, j: (0, mid(b, j)))),
            (vt, pl.BlockSpec((LANES, WINDOW), lambda b, j: (0, nxt(b, j))))]


def _ret_kernel(*refs, n, chunk, has_init, emit_state, bpb):
    r_ref, dec_ref, bd_ref = refs[:3]
    pos = 3
    if has_init:
        s0f_ref, s0b_ref = refs[pos:pos + 2]
        pos += 2
    o_ref = refs[pos]
    pos += 1
    if emit_state:
        sf_ref, sb_ref = refs[pos:pos + 2]
        pos += 2
    stf, stb, dsum, dec, cdec = refs[pos:pos + 5]
    nc = n // chunk
    n_slab = RET_HEADS // 2
    use_inter = has_init or nc > 1
    lane_lo = lax.broadcasted_iota(jnp.int32, (chunk, LANES), 1) < HEAD_DIM
    row_s = lax.broadcasted_iota(jnp.int32, (LANES, LANES), 0) < HEAD_DIM
    lane_s = lax.broadcasted_iota(jnp.int32, (LANES, LANES), 1) < HEAD_DIM
    blockdiag = row_s == lane_s

    @pl.when(pl.program_id(0) == 0)
    def _():
        x = dec_ref[...]
        lg = jnp.minimum(x, 0.0) - jnp.log(1.0 + jnp.exp(-jnp.abs(x)))
        rowc = lax.broadcasted_iota(jnp.int32, (chunk, LANES), 0).astype(F32)
        diff = (lax.broadcasted_iota(jnp.int32, (chunk, chunk), 0)
                - lax.broadcasted_iota(jnp.int32, (chunk, chunk), 1)).astype(F32)
        for hd in range(RET_HEADS):
            lf = lg[hd:hd + 1, 0:1]
            lb = lg[RET_HEADS + hd:RET_HEADS + hd + 1, 0:1]
            dsum[hd] = (jnp.where(diff >= 0, jnp.exp(jnp.maximum(diff, 0.0) * lf), 0.0)
                        + jnp.where(diff <= 0, jnp.exp(jnp.maximum(-diff, 0.0) * lb), 0.0))
        for sl in range(n_slab):
            f0, f1 = lg[2 * sl:2 * sl + 1], lg[2 * sl + 1:2 * sl + 2]
            b0 = lg[RET_HEADS + 2 * sl:RET_HEADS + 2 * sl + 1]
            b1 = lg[RET_HEADS + 2 * sl + 1:RET_HEADS + 2 * sl + 2]
            lf2 = jnp.where(lane_lo, f0, f1)
            lb2 = jnp.where(lane_lo, b0, b1)
            dec[0 + sl] = jnp.exp((chunk - 1.0 - rowc) * lf2)
            dec[2 + sl] = jnp.exp(rowc * lb2)
            dec[4 + sl] = jnp.exp((rowc + 1.0) * lf2)
            dec[6 + sl] = jnp.exp((chunk - rowc) * lb2)
            cdec[sl] = jnp.exp(chunk * jnp.where(row_s, f0, f1))
            cdec[n_slab + sl] = jnp.exp(chunk * jnp.where(row_s, b0, b1))

    lo, hi = slice(0, HEAD_DIM), slice(HEAD_DIM, LANES)
    for bi in range(bpb):
        for sl in range(n_slab):
            stf[bi, 0, sl] = jnp.zeros((LANES, LANES), F32)
            stb[bi, nc, sl] = jnp.zeros((LANES, LANES), F32)
            if has_init:
                stf[bi, 0, sl, lo, lo] = s0f_ref[bi, 2 * sl]
                stf[bi, 0, sl, hi, hi] = s0f_ref[bi, 2 * sl + 1]
                stb[bi, nc, sl, lo, lo] = s0b_ref[bi, 2 * sl]
                stb[bi, nc, sl, hi, hi] = s0b_ref[bi, 2 * sl + 1]

    def state_chain(bi, ci, sl):
        rows = _chunk_rows(bi * n, ci, chunk)
        yield
        k2 = r_ref[rows, 256 + sl * LANES:256 + (sl + 1) * LANES].astype(F32)
        kk = jnp.concatenate([(k2 * dec[0 + sl]).astype(BF16), (k2 * dec[2 + sl]).astype(BF16)], axis=1)
        yield
        u = _dot_tn(kk, r_ref[rows, 512 + sl * LANES:512 + (sl + 1) * LANES])
        yield
        stf[bi, ci + 1, sl] = jnp.where(blockdiag, u[:LANES], 0.0)
        stb[bi, ci, sl] = jnp.where(blockdiag, u[LANES:], 0.0)

    def output_chain(bi, ci, sl):
        rows = _chunk_rows(bi * n, ci, chunk)
        cols = lambda off: slice(off + sl * LANES, off + (sl + 1) * LANES)
        q2 = r_ref[rows, cols(0)]
        k2 = r_ref[rows, cols(256)]
        zero = jnp.zeros_like(q2)
        s_lo = _dot_nt(jnp.where(lane_lo, q2, zero), k2)
        s_hi = _dot_nt(jnp.where(lane_lo, zero, q2), k2)
        if use_inter:
            it = _dot(q2, jnp.concatenate([stf[bi, ci, sl], stb[bi, ci + 1, sl]], axis=1).astype(BF16))
        yield
        a_lo = (s_lo * dsum[2 * sl]).astype(BF16)
        a_hi = (s_hi * dsum[2 * sl + 1]).astype(BF16)
        yield
        v2 = r_ref[rows, cols(512)]
        o_lo, o_hi = _dot(a_lo, v2), _dot(a_hi, v2)
        yield
        o2 = jnp.where(lane_lo, o_lo, o_hi)
        if use_inter:
            o2 = o2 + it[:, :LANES] * dec[4 + sl] + it[:, LANES:] * dec[6 + sl]
        sq = (o2 * o2).astype(BF16)
        yield
        ss = _dot(sq, bd_ref[...])
        yield
        g2 = r_ref[rows, cols(768)].astype(F32)
        o_ref[rows, sl * LANES:(sl + 1) * LANES] = (o2 * lax.rsqrt(ss + EPS) * _silu(g2)).astype(BF16)

    def scan_states(bi):
        for sl in range(n_slab):
            for ci in range(nc):
                stf[bi, ci + 1, sl] = stf[bi, ci, sl] * cdec[sl] + stf[bi, ci + 1, sl]
            for ci in range(nc - 1, -1, -1):
                stb[bi, ci, sl] = stb[bi, ci + 1, sl] * cdec[n_slab + sl] + stb[bi, ci, sl]
            if emit_state:
                sf_ref[bi, 2 * sl] = stf[bi, nc, sl, lo, lo]
                sf_ref[bi, 2 * sl + 1] = stf[bi, nc, sl, hi, hi]
                sb_ref[bi, 2 * sl] = stb[bi, 0, sl, lo, lo]
                sb_ref[bi, 2 * sl + 1] = stb[bi, 0, sl, hi, hi]

    slabs = range(n_slab)
    if not use_inter:
        _run_skewed([state_chain(bi, 0, sl) for bi in range(bpb) for sl in slabs]
                    + [output_chain(bi, 0, sl) for bi in range(bpb) for sl in slabs])
        for bi in range(bpb):
            scan_states(bi)
    else:
        for bi in range(bpb):
            group = RET_CHUNK_GROUP if nc % RET_CHUNK_GROUP == 0 else 1

            def local_state(i, carry):
                _run_skewed([state_chain(bi, i * group + u, sl) for u in range(group) for sl in slabs])
                return carry

            def outputs(i, carry):
                _run_skewed([output_chain(bi, i * group + u, sl) for u in range(group) for sl in slabs])
                return carry

            lax.fori_loop(0, nc // group, local_state, 0)
            scan_states(bi)
            lax.fori_loop(0, nc // group, outputs, 0)


def _retention(r, dec_tab, bd, init, *, layer, bsz, n, emit_state, name, bpb=1):
    rw = r.shape[1]
    chunk = RET_CHUNK
    nc = n // chunk
    n_slab = RET_HEADS // 2
    has_init = init is not None
    in_specs = [pl.BlockSpec((bpb * n, rw), lambda b: (b, 0)),
                pl.BlockSpec((None, 8, LANES), lambda b: (layer, 0, 0)),
                pl.BlockSpec((LANES, LANES), lambda b: (0, 0))]
    args = [r, dec_tab, bd]
    if has_init:
        init_spec = pl.BlockSpec((bpb, None, RET_HEADS, HEAD_DIM, HEAD_DIM), lambda b: (b, layer, 0, 0, 0))
        in_specs += [init_spec, init_spec]
        args += list(init)
    ow = RET_HEADS * HEAD_DIM
    out_shape = [jax.ShapeDtypeStruct((bsz * n, ow), BF16)]
    out_specs = [pl.BlockSpec((bpb * n, ow), lambda b: (b, 0))]
    if emit_state:
        st_spec = pl.BlockSpec((bpb, RET_HEADS, HEAD_DIM, HEAD_DIM), lambda b: (b, 0, 0, 0))
        out_shape += [jax.ShapeDtypeStruct((bsz, RET_HEADS, HEAD_DIM, HEAD_DIM), F32)] * 2
        out_specs += [st_spec, st_spec]
    return pl.pallas_call(
        functools.partial(_ret_kernel, n=n, chunk=chunk, has_init=has_init, emit_state=emit_state, bpb=bpb),
        grid=(bsz // bpb,),
        in_specs=in_specs, out_specs=out_specs, out_shape=out_shape,
        scratch_shapes=[pltpu.VMEM((bpb, nc + 1, n_slab, LANES, LANES), F32),
                        pltpu.VMEM((bpb, nc + 1, n_slab, LANES, LANES), F32),
                        pltpu.VMEM((RET_HEADS, chunk, chunk), F32),
                        pltpu.VMEM((8, chunk, LANES), F32),
                        pltpu.VMEM((2 * n_slab, LANES, LANES), F32)],
        compiler_params=_cparams(("arbitrary",)),
        name=name,
    )(*args)


def _post_kernel(x_ref, og_ref, or_ref, ow_ref, mod_ref, wo_ref, wgu_ref, wd_ref,
                 gpost_ref, gpre2_ref, gpost2_ref, o_ref):
    m = mod_ref[0]
    gt1 = m[:, 2 * D_MODEL:3 * D_MODEL]
    sh2 = m[:, 3 * D_MODEL:4 * D_MODEL]
    sc2 = m[:, 4 * D_MODEL:5 * D_MODEL]
    gt2 = m[:, 5 * D_MODEL:6 * D_MODEL]
    sub = x_ref.shape[0] // POST_SUBTILES
    tiles = [slice(t * sub, (t + 1) * sub) for t in range(POST_SUBTILES)]

    def out_proj(rows):
        merged = jnp.concatenate([og_ref[rows, :], or_ref[rows, :], ow_ref[rows, :]], axis=1)
        return _dot(merged, wo_ref[...])

    def norms(rows, mixed):
        x1 = x_ref[rows, :] + gt1 * (_row_rms(mixed) * gpost_ref[...])
        return x1, (_row_rms(x1) * (gpre2_ref[...] * (1.0 + sc2)) + sh2).astype(BF16)

    def ffn(h2):
        ff = None
        for lo, hi in FF_CHUNKS:
            act = (_silu(_dot(h2, wgu_ref[:, lo:hi])) * _dot(h2, wgu_ref[:, D_FF + lo:D_FF + hi])).astype(BF16)
            part = _dot(act, wd_ref[lo:hi, :])
            ff = part if ff is None else ff + part
        return ff

    mixed = [out_proj(rows) for rows in tiles]
    x1s, ffs = [], []
    for rows, mx in zip(tiles, mixed):
        x1, h2 = norms(rows, mx)
        x1s.append(x1)
        ffs.append(ffn(h2))
    for rows, x1, ff in zip(tiles, x1s, ffs):
        o_ref[rows, :] = x1 + gt2 * (_row_rms(ff) * gpost2_ref[...])


def _post(x2d, og, orr, ow, mod, w_out_b, w_gu_b, w_down_b, gpost, gpre2, gpost2, *, layer, ctx, rows_per_batch):
    t_rows = x2d.shape[0]
    tm = TM_POST
    row = lambda i: (i, 0)
    lay3 = lambda i: (layer, 0, 0)
    gain = pl.BlockSpec((None, 1, D_MODEL), lay3)
    return pl.pallas_call(
        _post_kernel,
        grid=(t_rows // tm,),
        in_specs=[pl.BlockSpec((tm, D_MODEL), row),
                  pl.BlockSpec((tm, og.shape[1]), row),
                  pl.BlockSpec((tm, orr.shape[1]), row),
                  pl.BlockSpec((tm, ow.shape[1]), row),
                  _mod_spec(layer, ctx, rows_per_batch // tm),
                  _resident((None, D_MODEL, D_MODEL), lay3),
                  _resident((None, D_MODEL, 2 * D_FF), lay3),
                  _resident((None, D_FF, D_MODEL), lay3),
                  gain, gain, gain],
        out_specs=pl.BlockSpec((tm, D_MODEL), row),
        out_shape=jax.ShapeDtypeStruct((t_rows, D_MODEL), F32),
        compiler_params=_cparams(("parallel",)),
        name="post_mixer",
    )(x2d, og, orr, ow, mod, w_out_b, w_gu_b, w_down_b, gpost, gpre2, gpost2)


def _rope_tables(n):
    half = HEAD_DIM // 4
    pos = jnp.arange(n, dtype=jnp.int32)
    row = (pos // GRID_W).astype(F32)
    col = (pos % GRID_W).astype(F32)
    freqs = ROPE_BASE ** (-jnp.arange(half, dtype=F32) / half)
    ang_r = row[:, None] * freqs[None, :]
    ang_c = col[:, None] * freqs[None, :]
    ang = jnp.concatenate([ang_r, ang_r, ang_c, ang_c], axis=-1)
    second = (jnp.arange(HEAD_DIM) % (2 * half)) >= half
    cos = jnp.cos(ang)
    sin = jnp.sin(ang)
    sin_a = jnp.where(second[None, :], sin, 0.0)
    sin_b = jnp.where(second[None, :], 0.0, -sin)
    tile = lambda t: jnp.concatenate([t, t], axis=-1)
    return tile(cos), tile(sin_a), tile(sin_b)


def _cache_layouts(ck, cv):
    b, _, past = ck.shape[:3]
    k = jnp.transpose(ck.reshape(b, DEPTH, past, LANES), (1, 0, 2, 3)).reshape(DEPTH, b * past, LANES)
    vt = jnp.transpose(cv.reshape(b, DEPTH, past, LANES), (1, 3, 0, 2)).reshape(DEPTH, LANES, b * past)
    return k.astype(BF16), vt.astype(BF16)


def kernel(x_prompt, x_sample, cache_glob_k, cache_glob_v, state_ret_fwd, state_ret_bwd, cache_win_k, cache_win_v, c, c_ctx, w_mod, b_mod, g_pre_mix, g_post_mix, g_pre_ffn, g_post_ffn, w_in, g_q, g_k, ret_decay_fwd, ret_decay_bwd, win_sink, w_out, w_gate_up, w_down):
    bc, nc_, _ = x_prompt.shape
    bs, ns, _ = x_sample.shape
    past = cache_glob_k.shape[2]

    cond8 = jnp.concatenate([c, c_ctx[None, :], jnp.zeros((MOD_ROWS - bs - 1, D_MODEL), F32)], axis=0)
    mod = _modulation(cond8, w_mod, b_mod).reshape(DEPTH * MOD_ROWS, 1, 6 * D_MODEL)

    w_in_b = w_in.astype(BF16)
    w_out_b = w_out.astype(BF16)
    w_gu_b = w_gate_up.astype(BF16)
    w_down_b = w_down.astype(BF16)
    rope_tabs = _rope_tables(ns)
    seg = jnp.arange(LANES) // HEAD_DIM
    bd = jnp.where(seg[:, None] == seg[None, :], 1.0 / HEAD_DIM, 0.0).astype(BF16)
    gain3 = lambda g: g.reshape(DEPTH, 1, D_MODEL)
    gpre, gpost, gpre2, gpost2 = gain3(g_pre_mix), gain3(g_post_mix), gain3(g_pre_ffn), gain3(g_post_ffn)
    gq_t = jnp.tile(g_q, (1, 2)).reshape(DEPTH, 1, LANES)
    gk_t = jnp.tile(g_k, (1, 2)).reshape(DEPTH, 1, LANES)
    dec_tab = jnp.broadcast_to(jnp.concatenate([ret_decay_fwd, ret_decay_bwd], axis=1)[:, :, None],
                               (DEPTH, 2 * RET_HEADS, LANES))
    init = (state_ret_fwd, state_ret_bwd)
    cgk, cgvt = _cache_layouts(cache_glob_k, cache_glob_v)
    cwk, cwvt = _cache_layouts(cache_win_k, cache_win_v)

    y = x_prompt.reshape(bc * nc_, D_MODEL)
    z = x_sample.reshape(bs * ns, D_MODEL)
    new = [[] for _ in range(6)]
    for l in range(DEPTH):
        qg, kg, vgt, r, qw, kw, vwt, kg32, vg32, kw32, vw32 = _in_proj(
            y, mod, gpre, w_in_b, gq_t, gk_t, bd, None, layer=l, ctx=True, rows_per_batch=nc_)
        og = _attention(qg, _full_source(kg, vgt, nc_ * CTX_BPB), ("split",), None, l,
                        rows=nc_ * CTX_BPB, sub=nc_, stack=GROUP, n=nc_, name="attn_glob_ctx")
        ow = _attention(qw, _full_source(kw, vwt, nc_ * CTX_BPB), ("split",), win_sink, l,
                        rows=nc_ * CTX_BPB, sub=nc_, stack=GROUP, n=nc_, name="attn_win_ctx")
        orr, sf, sb = _retention(r, dec_tab, bd, None, layer=l, bsz=bc, n=nc_, emit_state=True, bpb=CTX_BPB,
                                 name="ret_ctx")
        y = _post(y, og, orr, ow, mod, w_out_b, w_gu_b, w_down_b, gpost, gpre2, gpost2,
                  layer=l, ctx=True, rows_per_batch=nc_)
        for lst, t in zip(new, (kg32, vg32, sf, sb, kw32, vw32)):
            lst.append(t)

        qg, kg, vgt, r, qw, kw, vwt = _in_proj(
            z, mod, gpre, w_in_b, gq_t, gk_t, bd, rope_tabs, layer=l, ctx=False, rows_per_batch=ns)
        og = _attention(qg, _full_source(kg, vgt, ns) + _cache_source(cgk, cgvt, l, past),
                        ("shared", "shared"), None, l,
                        rows=TQ_GLOB, sub=TQ_GLOB, stack=1, n=ns, name="attn_glob_smp")
        ow = _attention(qw, _band_source(kw, vwt, ns, TQ_WIN) + _cache_source(cwk, cwvt, l, past),
                        ("band", "shared"), win_sink, l,
                        rows=TQ_WIN, sub=WINDOW, stack=GLOB_HEADS, n=ns, name="attn_win_smp")
        (orr,) = _retention(r, dec_tab, bd, init, layer=l, bsz=bs, n=ns, emit_state=False, name="ret_smp")
        z = _post(z, og, orr, ow, mod, w_out_b, w_gu_b, w_down_b, gpost, gpre2, gpost2,
                  layer=l, ctx=False, rows_per_batch=ns)

    kv5 = lambda ts: jnp.transpose(jnp.stack(ts, axis=1).reshape(bc, DEPTH, KV_HEADS, HEAD_DIM, nc_),
                                   (0, 1, 4, 2, 3))
    st5 = lambda ts: jnp.stack(ts, axis=1)
    return (y.reshape(bc, nc_, D_MODEL), z.reshape(bs, ns, D_MODEL),
            kv5(new[0]), kv5(new[1]), st5(new[2]), st5(new[3]), kv5(new[4]), kv5(new[5]))
```

```python
import functools

import jax
import jax.numpy as jnp
from jax import lax
from jax.experimental import pallas as pl
from jax.experimental.pallas import tpu as pltpu

F32 = jnp.float32
BF16 = jnp.bfloat16

D_MODEL = 1024
DEPTH = 4
GRID_W = 64
HEAD_DIM = 64
LANES = 128
GLOB_HEADS = 6
RET_HEADS = 4
WIN_HEADS = 6
KV_HEADS = 2
GROUP = GLOB_HEADS // KV_HEADS
WINDOW = 128
ROPE_BASE = 10000.0
QK_SCALE = HEAD_DIM ** -0.5
LOG2E = 1.4426950408889634
D_FF = 2816
IN_WIDTH = 2304
EPS = 1e-6
NEG = -1e30
MOD_ROWS = 8
CTX_ROW = 4

GQ0, GK0, GV0 = 0, 384, 512
RET0 = 640
WQ0, WK0, WV0 = 1664, 2048, 2176

VMEM_LIMIT = 56 * 1024 * 1024

TM_IN = 1024
IN_SUBTILES = 4
TM_POST = 1024
POST_SUBTILES = 4
FF_CHUNKS = ((0, 1024), (1024, 2048), (2048, D_FF))
RET_CHUNK = 256
RET_CHUNK_GROUP = 2
CTX_BPB = 8
TQ_GLOB = 1024
TQ_WIN = 2048
ONES_ROWS = 16


def _cparams(sem):
    return pltpu.CompilerParams(dimension_semantics=sem, vmem_limit_bytes=VMEM_LIMIT)


def _resident(block_shape, index_map):
    return pl.BlockSpec(block_shape, index_map, pipeline_mode=pl.Buffered(1))


def _dot(a, b):
    return jnp.dot(a, b, preferred_element_type=F32)


def _dot_nt(a, b):
    return lax.dot_general(a, b, (((1,), (1,)), ((), ())), preferred_element_type=F32)


def _dot_tn(a, b):
    return lax.dot_general(a, b, (((0,), (0,)), ((), ())), preferred_element_type=F32)


def _silu(x):
    return x / (1.0 + jnp.exp(-x))


def _row_rms(x):
    return x * lax.rsqrt(jnp.mean(x * x, axis=-1, keepdims=True) + EPS)


def _chunk_rows(offset, ci, chunk):
    if isinstance(ci, int):
        return pl.ds(offset + ci * chunk, chunk)
    return pl.ds(pl.multiple_of(offset + ci * chunk, chunk), chunk)


def _run_skewed(chains):
    chains = list(chains)
    done = [False] * len(chains)
    t = 0
    while not all(done):
        live = [c for c in range(min(t + 1, len(chains))) if not done[c]]
        for c in sorted(live, key=lambda c: (t - c) % 2):
            try:
                next(chains[c])
            except StopIteration:
                done[c] = True
        t += 1


def _mod_kernel(c_ref, w_ref, b_ref, o_ref):
    s = _silu(c_ref[...]).astype(BF16)
    o_ref[0] = _dot(s, w_ref[0].astype(BF16)) + b_ref[0]


def _modulation(cond8, w_mod, b_mod):
    tn = 1536
    n_out = 6 * D_MODEL
    return pl.pallas_call(
        _mod_kernel,
        grid=(DEPTH, n_out // tn),
        in_specs=[pl.BlockSpec((MOD_ROWS, D_MODEL), lambda l, j: (0, 0)),
                  pl.BlockSpec((1, D_MODEL, tn), lambda l, j: (l, 0, j)),
                  pl.BlockSpec((1, 1, tn), lambda l, j: (l, 0, j))],
        out_specs=pl.BlockSpec((1, MOD_ROWS, tn), lambda l, j: (l, 0, j)),
        out_shape=jax.ShapeDtypeStruct((DEPTH, MOD_ROWS, n_out), F32),
        compiler_params=_cparams(("parallel", "parallel")),
        name="modulation",
    )(cond8, w_mod, b_mod.reshape(DEPTH, 1, n_out))


def _mod_spec(layer, ctx, tiles_per_batch):
    if ctx:
        return pl.BlockSpec((1, 1, 6 * D_MODEL), lambda i: (layer * MOD_ROWS + CTX_ROW, 0, 0))
    return pl.BlockSpec((1, 1, 6 * D_MODEL), lambda i: (layer * MOD_ROWS + i // tiles_per_batch, 0, 0))


def _in_proj_kernel(*refs, rope, ctx):
    x_ref, mod_ref, gpre_ref, w_ref, gq_ref, gk_ref, bd_ref = refs[:7]
    pos = 7
    if rope:
        cos_ref, sa_ref, sb_ref = refs[pos:pos + 3]
        pos += 3
    qg_ref, kg_ref, vgt_ref, r_ref, qw_ref, kw_ref, vwt_ref = refs[pos:pos + 7]
    pos += 7
    if ctx:
        kg32_ref, vg32_ref, kw32_ref, vw32_ref = refs[pos:pos + 4]

    m = mod_ref[0]
    sh1 = m[:, 0:D_MODEL]
    sc1 = m[:, D_MODEL:2 * D_MODEL]
    sub = x_ref.shape[0] // IN_SUBTILES
    lane_lo = lax.broadcasted_iota(jnp.int32, (sub, LANES), 1) < HEAD_DIM

    def prologue(rows):
        return (_row_rms(x_ref[rows, :]) * (gpre_ref[...] * (1.0 + sc1)) + sh1).astype(BF16)

    def head_norm(t, g):
        ss = _dot((t * t).astype(BF16), bd_ref[...])
        return t * lax.rsqrt(ss + EPS) * g

    def rot(t, rows):
        if not rope:
            return t
        return (t * cos_ref[rows, :] + pltpu.roll(t, 16, 1) * sa_ref[rows, :]
                + pltpu.roll(t, LANES - 16, 1) * sb_ref[rows, :])

    def store_q(slabs, q_ref, rows):
        for hd in range(GLOB_HEADS):
            t = slabs[hd // 2]
            kv = hd // GROUP
            if hd % 2 != kv:
                t = pltpu.roll(t, HEAD_DIM, 1)
            keep = lane_lo if kv == 0 else jnp.logical_not(lane_lo)
            q_ref[rows, hd * LANES:(hd + 1) * LANES] = jnp.where(keep, t * (QK_SCALE * LOG2E), 0.0).astype(BF16)

    def glob_epilogue(pg, rows):
        store_q([rot(head_norm(pg[:, s * LANES:(s + 1) * LANES], gq_ref[...]), rows) for s in range(3)],
                qg_ref, rows)
        kg = head_norm(pg[:, GK0:GK0 + LANES], gk_ref[...])
        vgt = pg[:, GV0:GV0 + LANES].T
        if ctx:
            kg32_ref[rows.start // sub] = kg.T
            vg32_ref[rows.start // sub] = vgt
        kg_ref[rows, :] = rot(kg, rows).astype(BF16)
        vgt_ref[:, rows] = vgt.astype(BF16)

    def ret_store(p, rows, lo, hi, scale=None):
        r_ref[rows, lo:hi] = (p if scale is None else p * scale).astype(BF16)

    def win_epilogue(pw, rows):
        store_q([rot(pw[:, s * LANES:(s + 1) * LANES], rows) for s in range(3)], qw_ref, rows)
        kw = pw[:, WK0 - WQ0:WK0 - WQ0 + LANES]
        vwt = pw[:, WV0 - WQ0:WV0 - WQ0 + LANES].T
        if ctx:
            kw32_ref[rows.start // sub] = kw.T
            vw32_ref[rows.start // sub] = vwt
        kw_ref[rows, :] = rot(kw, rows).astype(BF16)
        vwt_ref[:, rows] = vwt.astype(BF16)

    third = IN_WIDTH // 3
    rel = lambda c0: c0 - RET0

    def a_epilogue(pa, rows):
        glob_epilogue(pa[:, :RET0], rows)
        ret_store(pa[:, RET0:], rows, 0, rel(third))

    def b_epilogue(pb, rows):
        ret_store(pb[:, :LANES], rows, rel(third), 256)
        ret_store(pb[:, LANES:LANES + 256], rows, 256, 512, QK_SCALE)
        ret_store(pb[:, LANES + 256:], rows, 512, rel(2 * third))

    def c_epilogue(pc, rows):
        ret_store(pc[:, :WQ0 - 2 * third], rows, rel(2 * third), 1024)
        win_epilogue(pc[:, WQ0 - 2 * third:], rows)

    tiles = [slice(t * sub, (t + 1) * sub) for t in range(IN_SUBTILES)]
    hb = prologue(tiles[0])
    late = None
    for t, rows in enumerate(tiles):
        pa = _dot(hb, w_ref[:, 0:third])
        if late is not None:
            late()
        hb_next = prologue(tiles[t + 1]) if t + 1 < IN_SUBTILES else None
        pc = _dot(hb, w_ref[:, 2 * third:IN_WIDTH])
        a_epilogue(pa, rows)
        pb = _dot(hb, w_ref[:, third:2 * third])
        c_epilogue(pc, rows)
        late = functools.partial(b_epilogue, pb, rows)
        hb = hb_next
    late()


def _in_proj(x2d, mod, gpre, w_in_b, gq_t, gk_t, bd, rope_tabs, *, layer, ctx, rows_per_batch):
    t_rows = x2d.shape[0]
    tm = TM_IN
    tiles_per_batch = rows_per_batch // tm
    rope = rope_tabs is not None
    const2 = lambda i: (0, 0)
    lay3 = lambda i: (layer, 0, 0)
    row = lambda i: (i, 0)
    col = lambda i: (0, i)
    in_specs = [pl.BlockSpec((tm, D_MODEL), row),
                _mod_spec(layer, ctx, tiles_per_batch),
                pl.BlockSpec((None, 1, D_MODEL), lay3),
                _resident((D_MODEL, IN_WIDTH), const2),
                pl.BlockSpec((None, 1, LANES), lay3),
                pl.BlockSpec((None, 1, LANES), lay3),
                pl.BlockSpec((LANES, LANES), const2)]
    args = [x2d, mod, gpre, w_in_b, gq_t, gk_t, bd]
    if rope:
        tab = pl.BlockSpec((tm, LANES), lambda i: (i % tiles_per_batch, 0))
        in_specs += [tab, tab, tab]
        args += list(rope_tabs)
    bf = lambda shape: jax.ShapeDtypeStruct(shape, BF16)
    out_shape = [bf((t_rows, GLOB_HEADS * LANES)), bf((t_rows, LANES)), bf((LANES, t_rows)),
                 bf((t_rows, 4 * RET_HEADS * HEAD_DIM)),
                 bf((t_rows, WIN_HEADS * LANES)), bf((t_rows, LANES)), bf((LANES, t_rows))]
    out_specs = [pl.BlockSpec((tm, GLOB_HEADS * LANES), row), pl.BlockSpec((tm, LANES), row),
                 pl.BlockSpec((LANES, tm), col),
                 pl.BlockSpec((tm, 4 * RET_HEADS * HEAD_DIM), row),
                 pl.BlockSpec((tm, WIN_HEADS * LANES), row), pl.BlockSpec((tm, LANES), row),
                 pl.BlockSpec((LANES, tm), col)]
    if ctx:
        assert tm // IN_SUBTILES == rows_per_batch
        out_shape += [jax.ShapeDtypeStruct((t_rows // rows_per_batch, LANES, rows_per_batch), F32)] * 4
        out_specs += [pl.BlockSpec((IN_SUBTILES, LANES, rows_per_batch), lambda i: (i, 0, 0))] * 4
    return pl.pallas_call(
        functools.partial(_in_proj_kernel, rope=rope, ctx=ctx),
        grid=(t_rows // tm,),
        in_specs=in_specs, out_specs=out_specs, out_shape=out_shape,
        compiler_params=_cparams(("parallel",)),
        name="in_proj_ctx" if ctx else "in_proj_smp",
    )(*args)


def _attn_kernel(*refs, kinds, sink_layer, sub, n_lat, stack, cast):
    q_ref = refs[0]
    pos = 1
    srcs = []
    for kind in kinds:
        cnt = 6 if kind == "band" else 2
        srcs.append((kind, refs[pos:pos + cnt]))
        pos += cnt
    if sink_layer is not None:
        sink_ref = refs[pos]
        pos += 1
    if cast:
        refs[pos + 2][...] = refs[pos][...].astype(BF16)
        pos += 1
    o_ref = refs[pos]
    j = pl.program_id(1)
    nsub = q_ref.shape[0] // sub
    width = sub + 2 * WINDOW
    cols = stack * sub
    col_q = lax.broadcasted_iota(jnp.int32, (1, cols), 1)
    if "band" in kinds:
        krel = lax.broadcasted_iota(jnp.int32, (width, cols), 0) - WINDOW
        qrel = lax.broadcasted_iota(jnp.int32, (width, cols), 1) & (sub - 1)
        in_band = jnp.abs(krel - qrel) <= WINDOW

    def band_mask(bi):
        ok = in_band
        first = j * (nsub * sub) + bi * sub
        if bi == 0:
            ok = ok & (krel + first >= 0)
        if bi == nsub - 1:
            ok = ok & (krel + first < n_lat)
        return ok

    ones = lambda w: jnp.ones((ONES_ROWS, w), BF16)

    def key_rows(kind, r, bi):
        nk = r[0].shape[0]
        if kind == "split":
            return slice(bi * (nk // nsub), (bi + 1) * (nk // nsub))
        return slice(0, nk)

    band_k = {}
    band_vt = {}

    def keys(bi):
        out = []
        for kind, r in srcs:
            if kind == "band":
                if not band_k:
                    band_k[0] = jnp.concatenate([r[0][...], r[1][...], r[2][...]], axis=0)
                out.append(band_k[0][bi * sub:bi * sub + width, :])
            else:
                out.append(r[0][key_rows(kind, r, bi), :])
        return out

    def values(bi, kv):
        half = slice(HEAD_DIM * kv, HEAD_DIM * (kv + 1))
        out = []
        for kind, r in srcs:
            if kind == "band":
                if kv not in band_vt:
                    band_vt[kv] = jnp.concatenate([r[3][half, :], r[4][half, :], r[5][half, :]], axis=1)
                vt = band_vt[kv][:, bi * sub:bi * sub + width]
            else:
                vt = r[1][half, key_rows(kind, r, bi)]
            out.append(jnp.concatenate([vt, ones(vt.shape[1])], axis=0))
        return out

    def scores(bi, heads):
        rows = slice(bi * sub, (bi + 1) * sub)
        qs = [q_ref[rows, hd * LANES:(hd + 1) * LANES] for hd in heads]
        qs = qs[0] if len(qs) == 1 else jnp.concatenate(qs, axis=0)
        sts = []
        for (kind, _), k in zip(srcs, keys(bi)):
            st = _dot_nt(k, qs)
            sts.append(jnp.where(band_mask(bi), st, NEG) if kind == "band" else st)
        return sts

    def finish(bi, heads, sts):
        m = sts[0].max(axis=0, keepdims=True)
        for st in sts[1:]:
            m = jnp.maximum(m, st.max(axis=0, keepdims=True))
        if sink_layer is not None:
            sk = jnp.broadcast_to(sink_ref[sink_layer, heads[-1]] * LOG2E, (1, cols))
            for i in range(len(heads) - 2, -1, -1):
                sk = jnp.where(col_q < (i + 1) * sub, sink_ref[sink_layer, heads[i]] * LOG2E, sk)
            m = jnp.maximum(m, sk)
        ps = [jnp.exp2(st - m).astype(BF16) for st in sts]
        outs = []
        for kv in sorted({hd // GROUP for hd in heads}):
            mine = [i for i, hd in enumerate(heads) if hd // GROUP == kv]
            span = slice(mine[0] * sub, (mine[-1] + 1) * sub)
            acc = jnp.zeros((HEAD_DIM + ONES_ROWS, len(mine) * sub), F32)
            for p, vt in zip(ps, values(bi, kv)):
                acc = acc + _dot(vt, p[:, span])
            den = acc[HEAD_DIM:HEAD_DIM + 1, :]
            if sink_layer is not None:
                den = den + jnp.exp2(sk[:, span] - m[:, span])
            o = acc[:HEAD_DIM, :] / den
            outs += [o[:, i * sub:(i + 1) * sub] for i in range(len(mine))]
        return outs

    items = [(bi, list(range(g, g + stack))) for bi in range(nsub) for g in range(0, GLOB_HEADS, stack)]
    pieces = []
    pending = scores(*items[0])
    for i, (bi, heads) in enumerate(items):
        following = scores(*items[i + 1]) if i + 1 < len(items) else None
        pieces += finish(bi, heads, pending)
        pending = following

    for bi in range(nsub):
        for slab in range(GLOB_HEADS // 2):
            hd = bi * GLOB_HEADS + 2 * slab
            t = jnp.concatenate([pieces[hd], pieces[hd + 1]], axis=0)
            o_ref[bi * sub:(bi + 1) * sub, slab * LANES:(slab + 1) * LANES] = t.T.astype(BF16)


def _attention(q, sources, kinds, sink, sink_layer, *, rows, sub, stack, n, name, cast=None):
    t_rows = q.shape[0]
    nq = max(n // rows, 1)
    steps = t_rows // rows
    in_specs = [pl.BlockSpec((rows, q.shape[1]), lambda b, j: (b * nq + j, 0))]
    args = [q]
    for arr, spec in sources:
        in_specs.append(spec)
        args.append(arr)
    if sink is not None:
        in_specs.append(pl.BlockSpec(memory_space=pltpu.SMEM))
        args.append(sink)
    ow = GLOB_HEADS * HEAD_DIM
    out_specs = [pl.BlockSpec((rows, ow), lambda b, j: (b * nq + j, 0))]
    out_shape = [jax.ShapeDtypeStruct((t_rows, ow), BF16)]
    if cast is not None:
        w, w_layer = cast
        _, wr, wc = w.shape
        in_specs.append(pl.BlockSpec((None, wr // steps, wc), lambda b, j: (w_layer, b * nq + j, 0)))
        args.append(w)
        out_specs.append(pl.BlockSpec((wr // steps, wc), lambda b, j: (b * nq + j, 0)))
        out_shape.append(jax.ShapeDtypeStruct((wr, wc), BF16))
    outs = pl.pallas_call(
        functools.partial(_attn_kernel, kinds=kinds, sink_layer=sink_layer if sink is not None else None,
                          sub=sub, n_lat=n, stack=stack, cast=cast is not None),
        grid=(steps // nq, nq),
        in_specs=in_specs, out_specs=out_specs, out_shape=out_shape,
        compiler_params=_cparams(("parallel", "parallel")),
        name=name,
    )(*args)
    return outs if cast is not None else outs[0]


def _full_source(k, vt, n):
    return [(k, pl.BlockSpec((n, LANES), lambda b, j: (b, 0))),
            (vt, pl.BlockSpec((LANES, n), lambda b, j: (0, b)))]


def _cache_source(k_all, vt_all, layer, past):
    return [(k_all, pl.BlockSpec((None, past, LANES), lambda b, j: (layer, b, 0))),
            (vt_all, pl.BlockSpec((None, LANES, past), lambda b, j: (layer, 0, b)))]


def _band_source(k, vt, n, tq):
    per = n // WINDOW
    step = tq // WINDOW
    prev = lambda b, j: b * per + jnp.maximum(step * j - 1, 0)
    nxt = lambda b, j: b * per + jnp.minimum(step * j + step, per - 1)
    mid = lambda b, j: b * (n // tq) + j
    return [(k, pl.BlockSpec((WINDOW, LANES), lambda b, j: (prev(b, j), 0))),
            (k, pl.BlockSpec((tq, LANES), lambda b, j: (mid(b, j), 0))),
            (k, pl.BlockSpec((WINDOW, LANES), lambda b, j: (nxt(b, j), 0))),
            (vt, pl.BlockSpec((LANES, WINDOW), lambda b, j: (0, prev(b, j)))),
            (vt, pl.BlockSpec((LANES, tq), lambda b, j: (0, mid(b, j)))),
            (vt, pl.BlockSpec((LANES, WINDOW), lambda b, j: (0, nxt(b, j))))]


def _ret_kernel(*refs, n, chunk, has_init, emit_state, bpb):
    r_ref, dec_ref, bd_ref = refs[:3]
    pos = 3
    if has_init:
        s0f_ref, s0b_ref = refs[pos:pos + 2]
        pos += 2
    o_ref = refs[pos]
    pos += 1
    if emit_state:
        sf_ref, sb_ref = refs[pos:pos + 2]
        pos += 2
    stf, stb, dsum, dec, cdec = refs[pos:pos + 5]
    nc = n // chunk
    n_slab = RET_HEADS // 2
    use_inter = has_init or nc > 1
    lane_lo = lax.broadcasted_iota(jnp.int32, (chunk, LANES), 1) < HEAD_DIM
    row_s = lax.broadcasted_iota(jnp.int32, (LANES, LANES), 0) < HEAD_DIM
    lane_s = lax.broadcasted_iota(jnp.int32, (LANES, LANES), 1) < HEAD_DIM
    blockdiag = row_s == lane_s

    @pl.when(pl.program_id(0) == 0)
    def _():
        x = dec_ref[...]
        lg = jnp.minimum(x, 0.0) - jnp.log(1.0 + jnp.exp(-jnp.abs(x)))
        rowc = lax.broadcasted_iota(jnp.int32, (chunk, LANES), 0).astype(F32)
        diff = (lax.broadcasted_iota(jnp.int32, (chunk, chunk), 0)
                - lax.broadcasted_iota(jnp.int32, (chunk, chunk), 1)).astype(F32)
        for hd in range(RET_HEADS):
            lf = lg[hd:hd + 1, 0:1]
            lb = lg[RET_HEADS + hd:RET_HEADS + hd + 1, 0:1]
            dsum[hd] = (jnp.where(diff >= 0, jnp.exp(jnp.maximum(diff, 0.0) * lf), 0.0)
                        + jnp.where(diff <= 0, jnp.exp(jnp.maximum(-diff, 0.0) * lb), 0.0))
        for sl in range(n_slab):
            f0, f1 = lg[2 * sl:2 * sl + 1], lg[2 * sl + 1:2 * sl + 2]
            b0 = lg[RET_HEADS + 2 * sl:RET_HEADS + 2 * sl + 1]
            b1 = lg[RET_HEADS + 2 * sl + 1:RET_HEADS + 2 * sl + 2]
            lf2 = jnp.where(lane_lo, f0, f1)
            lb2 = jnp.where(lane_lo, b0, b1)
            dec[0 + sl] = jnp.exp((chunk - 1.0 - rowc) * lf2)
            dec[2 + sl] = jnp.exp(rowc * lb2)
            dec[4 + sl] = jnp.exp((rowc + 1.0) * lf2)
            dec[6 + sl] = jnp.exp((chunk - rowc) * lb2)
            cdec[sl] = jnp.exp(chunk * jnp.where(row_s, f0, f1))
            cdec[n_slab + sl] = jnp.exp(chunk * jnp.where(row_s, b0, b1))

    lo, hi = slice(0, HEAD_DIM), slice(HEAD_DIM, LANES)
    for bi in range(bpb):
        for sl in range(n_slab):
            stf[bi, 0, sl] = jnp.zeros((LANES, LANES), F32)
            stb[bi, nc, sl] = jnp.zeros((LANES, LANES), F32)
            if has_init:
                stf[bi, 0, sl, lo, lo] = s0f_ref[bi, 2 * sl]
                stf[bi, 0, sl, hi, hi] = s0f_ref[bi, 2 * sl + 1]
                stb[bi, nc, sl, lo, lo] = s0b_ref[bi, 2 * sl]
                stb[bi, nc, sl, hi, hi] = s0b_ref[bi, 2 * sl + 1]

    def state_chain(bi, ci, sl):
        rows = _chunk_rows(bi * n, ci, chunk)
        yield
        k2 = r_ref[rows, 256 + sl * LANES:256 + (sl + 1) * LANES].astype(F32)
        kk = jnp.concatenate([(k2 * dec[0 + sl]).astype(BF16), (k2 * dec[2 + sl]).astype(BF16)], axis=1)
        yield
        u = _dot_tn(kk, r_ref[rows, 512 + sl * LANES:512 + (sl + 1) * LANES])
        yield
        stf[bi, ci + 1, sl] = jnp.where(blockdiag, u[:LANES], 0.0)
        stb[bi, ci, sl] = jnp.where(blockdiag, u[LANES:], 0.0)

    def output_chain(bi, ci, sl):
        rows = _chunk_rows(bi * n, ci, chunk)
        cols = lambda off: slice(off + sl * LANES, off + (sl + 1) * LANES)
        q2 = r_ref[rows, cols(0)]
        k2 = r_ref[rows, cols(256)]
        zero = jnp.zeros_like(q2)
        s_lo = _dot_nt(jnp.where(lane_lo, q2, zero), k2)
        s_hi = _dot_nt(jnp.where(lane_lo, zero, q2), k2)
        if use_inter:
            it = _dot(q2, jnp.concatenate([stf[bi, ci, sl], stb[bi, ci + 1, sl]], axis=1).astype(BF16))
        yield
        a_lo = (s_lo * dsum[2 * sl]).astype(BF16)
        a_hi = (s_hi * dsum[2 * sl + 1]).astype(BF16)
        yield
        v2 = r_ref[rows, cols(512)]
        o_lo, o_hi = _dot(a_lo, v2), _dot(a_hi, v2)
        yield
        o2 = jnp.where(lane_lo, o_lo, o_hi)
        if use_inter:
            o2 = o2 + it[:, :LANES] * dec[4 + sl] + it[:, LANES:] * dec[6 + sl]
        sq = (o2 * o2).astype(BF16)
        yield
        ss = _dot(sq, bd_ref[...])
        yield
        g2 = r_ref[rows, cols(768)].astype(F32)
        o_ref[rows, sl * LANES:(sl + 1) * LANES] = (o2 * lax.rsqrt(ss + EPS) * _silu(g2)).astype(BF16)

    def scan_states(bi):
        for sl in range(n_slab):
            for ci in range(nc):
                stf[bi, ci + 1, sl] = stf[bi, ci, sl] * cdec[sl] + stf[bi, ci + 1, sl]
            for ci in range(nc - 1, -1, -1):
                stb[bi, ci, sl] = stb[bi, ci + 1, sl] * cdec[n_slab + sl] + stb[bi, ci, sl]
            if emit_state:
                sf_ref[bi, 2 * sl] = stf[bi, nc, sl, lo, lo]
                sf_ref[bi, 2 * sl + 1] = stf[bi, nc, sl, hi, hi]
                sb_ref[bi, 2 * sl] = stb[bi, 0, sl, lo, lo]
                sb_ref[bi, 2 * sl + 1] = stb[bi, 0, sl, hi, hi]

    slabs = range(n_slab)
    if not use_inter:
        _run_skewed([state_chain(bi, 0, sl) for bi in range(bpb) for sl in slabs]
                    + [output_chain(bi, 0, sl) for bi in range(bpb) for sl in slabs])
        for bi in range(bpb):
            scan_states(bi)
    else:
        for bi in range(bpb):
            group = RET_CHUNK_GROUP if nc % RET_CHUNK_GROUP == 0 else 1

            def local_state(i, carry):
                _run_skewed([state_chain(bi, i * group + u, sl) for u in range(group) for sl in slabs])
                return carry

            def outputs(i, carry):
                _run_skewed([output_chain(bi, i * group + u, sl) for u in range(group) for sl in slabs])
                return carry

            lax.fori_loop(0, nc // group, local_state, 0)
            scan_states(bi)
            lax.fori_loop(0, nc // group, outputs, 0)


def _retention(r, dec_tab, bd, init, *, layer, bsz, n, emit_state, name, bpb=1):
    rw = r.shape[1]
    chunk = RET_CHUNK
    nc = n // chunk
    n_slab = RET_HEADS // 2
    has_init = init is not None
    in_specs = [pl.BlockSpec((bpb * n, rw), lambda b: (b, 0)),
                pl.BlockSpec((None, 8, LANES), lambda b: (layer, 0, 0)),
                pl.BlockSpec((LANES, LANES), lambda b: (0, 0))]
    args = [r, dec_tab, bd]
    if has_init:
        init_spec = pl.BlockSpec((bpb, None, RET_HEADS, HEAD_DIM, HEAD_DIM), lambda b: (b, layer, 0, 0, 0))
        in_specs += [init_spec, init_spec]
        args += list(init)
    ow = RET_HEADS * HEAD_DIM
    out_shape = [jax.ShapeDtypeStruct((bsz * n, ow), BF16)]
    out_specs = [pl.BlockSpec((bpb * n, ow), lambda b: (b, 0))]
    if emit_state:
        st_spec = pl.BlockSpec((bpb, RET_HEADS, HEAD_DIM, HEAD_DIM), lambda b: (b, 0, 0, 0))
        out_shape += [jax.ShapeDtypeStruct((bsz, RET_HEADS, HEAD_DIM, HEAD_DIM), F32)] * 2
        out_specs += [st_spec, st_spec]
    return pl.pallas_call(
        functools.partial(_ret_kernel, n=n, chunk=chunk, has_init=has_init, emit_state=emit_state, bpb=bpb),
        grid=(bsz // bpb,),
        in_specs=in_specs, out_specs=out_specs, out_shape=out_shape,
        scratch_shapes=[pltpu.VMEM((bpb, nc + 1, n_slab, LANES, LANES), F32),
                        pltpu.VMEM((bpb, nc + 1, n_slab, LANES, LANES), F32),
                        pltpu.VMEM((RET_HEADS, chunk, chunk), F32),
                        pltpu.VMEM((8, chunk, LANES), F32),
                        pltpu.VMEM((2 * n_slab, LANES, LANES), F32)],
        compiler_params=_cparams(("arbitrary",)),
        name=name,
    )(*args)


def _post_kernel(x_ref, og_ref, or_ref, ow_ref, mod_ref, wo_ref, wgu_ref, wd_ref,
                 gpost_ref, gpre2_ref, gpost2_ref, o_ref):
    m = mod_ref[0]
    gt1 = m[:, 2 * D_MODEL:3 * D_MODEL]
    sh2 = m[:, 3 * D_MODEL:4 * D_MODEL]
    sc2 = m[:, 4 * D_MODEL:5 * D_MODEL]
    gt2 = m[:, 5 * D_MODEL:6 * D_MODEL]
    sub = x_ref.shape[0] // POST_SUBTILES
    tiles = [slice(t * sub, (t + 1) * sub) for t in range(POST_SUBTILES)]

    def out_proj(rows):
        merged = jnp.concatenate([og_ref[rows, :], or_ref[rows, :], ow_ref[rows, :]], axis=1)
        return _dot(merged, wo_ref[...])

    def norms(rows, mixed):
        x1 = x_ref[rows, :] + gt1 * (_row_rms(mixed) * gpost_ref[...])
        return x1, (_row_rms(x1) * (gpre2_ref[...] * (1.0 + sc2)) + sh2).astype(BF16)

    def ffn(h2):
        ff = None
        for lo, hi in FF_CHUNKS:
            act = (_silu(_dot(h2, wgu_ref[:, lo:hi])) * _dot(h2, wgu_ref[:, D_FF + lo:D_FF + hi])).astype(BF16)
            part = _dot(act, wd_ref[lo:hi, :])
            ff = part if ff is None else ff + part
        return ff

    mixed = [out_proj(rows) for rows in tiles]
    x1s, ffs = [], []
    for rows, mx in zip(tiles, mixed):
        x1, h2 = norms(rows, mx)
        x1s.append(x1)
        ffs.append(ffn(h2))
    for rows, x1, ff in zip(tiles, x1s, ffs):
        o_ref[rows, :] = x1 + gt2 * (_row_rms(ff) * gpost2_ref[...])


def _post(x2d, og, orr, ow, mod, w_out_b, w_gu_b, w_down_b, gpost, gpre2, gpost2, *, layer, ctx, rows_per_batch):
    t_rows = x2d.shape[0]
    tm = TM_POST
    row = lambda i: (i, 0)
    lay3 = lambda i: (layer, 0, 0)
    gain = pl.BlockSpec((None, 1, D_MODEL), lay3)
    return pl.pallas_call(
        _post_kernel,
        grid=(t_rows // tm,),
        in_specs=[pl.BlockSpec((tm, D_MODEL), row),
                  pl.BlockSpec((tm, og.shape[1]), row),
                  pl.BlockSpec((tm, orr.shape[1]), row),
                  pl.BlockSpec((tm, ow.shape[1]), row),
                  _mod_spec(layer, ctx, rows_per_batch // tm),
                  _resident((D_MODEL, D_MODEL), lambda i: (0, 0)),
                  _resident((D_MODEL, 2 * D_FF), lambda i: (0, 0)),
                  _resident((D_FF, D_MODEL), lambda i: (0, 0)),
                  gain, gain, gain],
        out_specs=pl.BlockSpec((tm, D_MODEL), row),
        out_shape=jax.ShapeDtypeStruct((t_rows, D_MODEL), F32),
        compiler_params=_cparams(("parallel",)),
        name="post_mixer",
    )(x2d, og, orr, ow, mod, w_out_b, w_gu_b, w_down_b, gpost, gpre2, gpost2)


def _rope_tables(n):
    half = HEAD_DIM // 4
    pos = jnp.arange(n, dtype=jnp.int32)
    row = (pos // GRID_W).astype(F32)
    col = (pos % GRID_W).astype(F32)
    freqs = ROPE_BASE ** (-jnp.arange(half, dtype=F32) / half)
    ang_r = row[:, None] * freqs[None, :]
    ang_c = col[:, None] * freqs[None, :]
    ang = jnp.concatenate([ang_r, ang_r, ang_c, ang_c], axis=-1)
    second = (jnp.arange(HEAD_DIM) % (2 * half)) >= half
    cos = jnp.cos(ang)
    sin = jnp.sin(ang)
    sin_a = jnp.where(second[None, :], sin, 0.0)
    sin_b = jnp.where(second[None, :], 0.0, -sin)
    tile = lambda t: jnp.concatenate([t, t], axis=-1)
    return tile(cos), tile(sin_a), tile(sin_b)


def _cache_layouts(ck, cv):
    b, _, past = ck.shape[:3]
    k = jnp.transpose(ck.reshape(b, DEPTH, past, LANES), (1, 0, 2, 3)).reshape(DEPTH, b * past, LANES)
    vt = jnp.transpose(cv.reshape(b, DEPTH, past, LANES), (1, 3, 0, 2)).reshape(DEPTH, LANES, b * past)
    return k.astype(BF16), vt.astype(BF16)


def kernel(x_prompt, x_sample, cache_glob_k, cache_glob_v, state_ret_fwd, state_ret_bwd, cache_win_k, cache_win_v, c, c_ctx, w_mod, b_mod, g_pre_mix, g_post_mix, g_pre_ffn, g_post_ffn, w_in, g_q, g_k, ret_decay_fwd, ret_decay_bwd, win_sink, w_out, w_gate_up, w_down):
    bc, nc_, _ = x_prompt.shape
    bs, ns, _ = x_sample.shape
    past = cache_glob_k.shape[2]

    cond8 = jnp.concatenate([c, c_ctx[None, :], jnp.zeros((MOD_ROWS - bs - 1, D_MODEL), F32)], axis=0)
    mod = _modulation(cond8, w_mod, b_mod).reshape(DEPTH * MOD_ROWS, 1, 6 * D_MODEL)

    w_in_b, w_out_b, w_gu_b, w_down_b = (w[0].astype(BF16) for w in (w_in, w_out, w_gate_up, w_down))
    rope_tabs = _rope_tables(ns)
    seg = jnp.arange(LANES) // HEAD_DIM
    bd = jnp.where(seg[:, None] == seg[None, :], 1.0 / HEAD_DIM, 0.0).astype(BF16)
    gain3 = lambda g: g.reshape(DEPTH, 1, D_MODEL)
    gpre, gpost, gpre2, gpost2 = gain3(g_pre_mix), gain3(g_post_mix), gain3(g_pre_ffn), gain3(g_post_ffn)
    gq_t = jnp.tile(g_q, (1, 2)).reshape(DEPTH, 1, LANES)
    gk_t = jnp.tile(g_k, (1, 2)).reshape(DEPTH, 1, LANES)
    dec_tab = jnp.broadcast_to(jnp.concatenate([ret_decay_fwd, ret_decay_bwd], axis=1)[:, :, None],
                               (DEPTH, 2 * RET_HEADS, LANES))
    init = (state_ret_fwd, state_ret_bwd)
    cgk, cgvt = _cache_layouts(cache_glob_k, cache_glob_v)
    cwk, cwvt = _cache_layouts(cache_win_k, cache_win_v)

    y = x_prompt.reshape(bc * nc_, D_MODEL)
    z = x_sample.reshape(bs * ns, D_MODEL)
    new = [[] for _ in range(6)]
    for l in range(DEPTH):
        qg, kg, vgt, r, qw, kw, vwt, kg32, vg32, kw32, vw32 = _in_proj(
            y, mod, gpre, w_in_b, gq_t, gk_t, bd, None, layer=l, ctx=True, rows_per_batch=nc_)
        nxt = l + 1 if l + 1 < DEPTH else None
        cast = lambda w: None if nxt is None else (w, nxt)
        og = _attention(qg, _full_source(kg, vgt, nc_ * CTX_BPB), ("split",), None, l, cast=cast(w_in),
                        rows=nc_ * CTX_BPB, sub=nc_, stack=GROUP, n=nc_, name="attn_glob_ctx")
        ow = _attention(qw, _full_source(kw, vwt, nc_ * CTX_BPB), ("split",), win_sink, l, cast=cast(w_out),
                        rows=nc_ * CTX_BPB, sub=nc_, stack=GROUP, n=nc_, name="attn_win_ctx")
        if nxt is not None:
            (og, w_in_next), (ow, w_out_next) = og, ow
        orr, sf, sb = _retention(r, dec_tab, bd, None, layer=l, bsz=bc, n=nc_, emit_state=True, bpb=CTX_BPB,
                                 name="ret_ctx")
        y = _post(y, og, orr, ow, mod, w_out_b, w_gu_b, w_down_b, gpost, gpre2, gpost2,
                  layer=l, ctx=True, rows_per_batch=nc_)
        for lst, t in zip(new, (kg32, vg32, sf, sb, kw32, vw32)):
            lst.append(t)

        qg, kg, vgt, r, qw, kw, vwt = _in_proj(
            z, mod, gpre, w_in_b, gq_t, gk_t, bd, rope_tabs, layer=l, ctx=False, rows_per_batch=ns)
        og = _attention(qg, _full_source(kg, vgt, ns) + _cache_source(cgk, cgvt, l, past),
                        ("shared", "shared"), None, l, cast=cast(w_gate_up),
                        rows=TQ_GLOB, sub=TQ_GLOB, stack=1, n=ns, name="attn_glob_smp")
        ow = _attention(qw, _band_source(kw, vwt, ns, TQ_WIN) + _cache_source(cwk, cwvt, l, past),
                        ("band", "shared"), win_sink, l, cast=cast(w_down),
                        rows=TQ_WIN, sub=WINDOW, stack=GLOB_HEADS, n=ns, name="attn_win_smp")
        if nxt is not None:
            (og, w_gu_next), (ow, w_down_next) = og, ow
        (orr,) = _retention(r, dec_tab, bd, init, layer=l, bsz=bs, n=ns, emit_state=False, name="ret_smp")
        z = _post(z, og, orr, ow, mod, w_out_b, w_gu_b, w_down_b, gpost, gpre2, gpost2,
                  layer=l, ctx=False, rows_per_batch=ns)
        if nxt is not None:
            w_in_b, w_out_b, w_gu_b, w_down_b = w_in_next, w_out_next, w_gu_next, w_down_next

    kv5 = lambda ts: jnp.transpose(jnp.stack(ts, axis=1).reshape(bc, DEPTH, KV_HEADS, HEAD_DIM, nc_),
                                   (0, 1, 4, 2, 3))
    st5 = lambda ts: jnp.stack(ts, axis=1)
    return (y.reshape(bc, nc_, D_MODEL), z.reshape(bs, ns, D_MODEL),
            kv5(new[0]), kv5(new[1]), st5(new[2]), st5(new[3]), kv5(new[4]), kv5(new[5]))
```

```python
import functools

import jax
import jax.numpy as jnp
from jax import lax
from jax.experimental import pallas as pl
from jax.experimental.pallas import tpu as pltpu

F32 = jnp.float32
BF16 = jnp.bfloat16

D_MODEL = 1024
DEPTH = 4
GRID_W = 64
HEAD_DIM = 64
LANES = 128
GLOB_HEADS = 6
RET_HEADS = 4
WIN_HEADS = 6
KV_HEADS = 2
GROUP = GLOB_HEADS // KV_HEADS
WINDOW = 128
ROPE_BASE = 10000.0
QK_SCALE = HEAD_DIM ** -0.5
LOG2E = 1.4426950408889634
D_FF = 2816
IN_WIDTH = 2304
EPS = 1e-6
NEG = -1e30
MOD_ROWS = 8
CTX_ROW = 4

GQ0, GK0, GV0 = 0, 384, 512
RET0 = 640
WQ0, WK0, WV0 = 1664, 2048, 2176

VMEM_LIMIT = 56 * 1024 * 1024

TM_IN = 1024
IN_SUBTILES = 4
TM_POST = 1024
POST_SUBTILES = 2
FF_CHUNKS = ((0, 1024), (1024, 2048), (2048, D_FF))
RET_CHUNK = 256
RET_CHUNK_GROUP = 2
CTX_BPB = 8
TQ_GLOB = 1024
TQ_WIN = 2048
ONES_ROWS = 16


def _cparams(sem):
    return pltpu.CompilerParams(dimension_semantics=sem, vmem_limit_bytes=VMEM_LIMIT)


def _resident(block_shape, index_map):
    return pl.BlockSpec(block_shape, index_map, pipeline_mode=pl.Buffered(1))


def _dot(a, b):
    return jnp.dot(a, b, preferred_element_type=F32)


def _dot_nt(a, b):
    return lax.dot_general(a, b, (((1,), (1,)), ((), ())), preferred_element_type=F32)


def _dot_tn(a, b):
    return lax.dot_general(a, b, (((0,), (0,)), ((), ())), preferred_element_type=F32)


def _silu(x):
    return x / (1.0 + jnp.exp(-x))


def _row_rms(x):
    return x * lax.rsqrt(jnp.mean(x * x, axis=-1, keepdims=True) + EPS)


def _chunk_rows(offset, ci, chunk):
    if isinstance(ci, int):
        return pl.ds(offset + ci * chunk, chunk)
    return pl.ds(pl.multiple_of(offset + ci * chunk, chunk), chunk)


def _run_skewed(chains):
    chains = list(chains)
    done = [False] * len(chains)
    t = 0
    while not all(done):
        live = [c for c in range(min(t + 1, len(chains))) if not done[c]]
        for c in sorted(live, key=lambda c: (t - c) % 2):
            try:
                next(chains[c])
            except StopIteration:
                done[c] = True
        t += 1


def _mod_kernel(c_ref, w_ref, b_ref, o_ref):
    s = _silu(c_ref[...]).astype(BF16)
    o_ref[0] = _dot(s, w_ref[0].astype(BF16)) + b_ref[0]


def _modulation(cond8, w_mod, b_mod):
    tn = 1536
    n_out = 6 * D_MODEL
    return pl.pallas_call(
        _mod_kernel,
        grid=(DEPTH, n_out // tn),
        in_specs=[pl.BlockSpec((MOD_ROWS, D_MODEL), lambda l, j: (0, 0)),
                  pl.BlockSpec((1, D_MODEL, tn), lambda l, j: (l, 0, j)),
                  pl.BlockSpec((1, 1, tn), lambda l, j: (l, 0, j))],
        out_specs=pl.BlockSpec((1, MOD_ROWS, tn), lambda l, j: (l, 0, j)),
        out_shape=jax.ShapeDtypeStruct((DEPTH, MOD_ROWS, n_out), F32),
        compiler_params=_cparams(("parallel", "parallel")),
        name="modulation",
    )(cond8, w_mod, b_mod.reshape(DEPTH, 1, n_out))


def _mod_spec(layer, ctx, tiles_per_batch):
    if ctx:
        return pl.BlockSpec((1, 1, 6 * D_MODEL), lambda i: (layer * MOD_ROWS + CTX_ROW, 0, 0))
    return pl.BlockSpec((1, 1, 6 * D_MODEL), lambda i: (layer * MOD_ROWS + i // tiles_per_batch, 0, 0))


def _in_proj_kernel(*refs, rope, ctx):
    x_ref, mod_ref, gpre_ref, w_ref, gq_ref, gk_ref, bd_ref = refs[:7]
    pos = 7
    if rope:
        cos_ref, sa_ref, sb_ref = refs[pos:pos + 3]
        pos += 3
    qg_ref, kg_ref, vgt_ref, r_ref, qw_ref, kw_ref, vwt_ref = refs[pos:pos + 7]
    pos += 7
    if ctx:
        kg32_ref, vg32_ref, kw32_ref, vw32_ref = refs[pos:pos + 4]

    m = mod_ref[0]
    sh1 = m[:, 0:D_MODEL]
    sc1 = m[:, D_MODEL:2 * D_MODEL]
    sub = x_ref.shape[0] // IN_SUBTILES
    lane_lo = lax.broadcasted_iota(jnp.int32, (sub, LANES), 1) < HEAD_DIM

    def prologue(rows):
        return (_row_rms(x_ref[rows, :]) * (gpre_ref[...] * (1.0 + sc1)) + sh1).astype(BF16)

    def head_norm(t, g):
        ss = _dot((t * t).astype(BF16), bd_ref[...])
        return t * lax.rsqrt(ss + EPS) * g

    def rot(t, rows):
        if not rope:
            return t
        return (t * cos_ref[rows, :] + pltpu.roll(t, 16, 1) * sa_ref[rows, :]
                + pltpu.roll(t, LANES - 16, 1) * sb_ref[rows, :])

    def store_q(slabs, q_ref, rows):
        for hd in range(GLOB_HEADS):
            t = slabs[hd // 2]
            kv = hd // GROUP
            if hd % 2 != kv:
                t = pltpu.roll(t, HEAD_DIM, 1)
            keep = lane_lo if kv == 0 else jnp.logical_not(lane_lo)
            q_ref[rows, hd * LANES:(hd + 1) * LANES] = jnp.where(keep, t * (QK_SCALE * LOG2E), 0.0).astype(BF16)

    def glob_epilogue(pg, rows):
        store_q([rot(head_norm(pg[:, s * LANES:(s + 1) * LANES], gq_ref[...]), rows) for s in range(3)],
                qg_ref, rows)
        kg = head_norm(pg[:, GK0:GK0 + LANES], gk_ref[...])
        vgt = pg[:, GV0:GV0 + LANES].T
        if ctx:
            kg32_ref[rows.start // sub] = kg.T
            vg32_ref[rows.start // sub] = vgt
        kg_ref[rows, :] = rot(kg, rows).astype(BF16)
        vgt_ref[:, rows] = vgt.astype(BF16)

    def ret_store(p, rows, lo, hi, scale=None):
        r_ref[rows, lo:hi] = (p if scale is None else p * scale).astype(BF16)

    def win_epilogue(pw, rows):
        store_q([rot(pw[:, s * LANES:(s + 1) * LANES], rows) for s in range(3)], qw_ref, rows)
        kw = pw[:, WK0 - WQ0:WK0 - WQ0 + LANES]
        vwt = pw[:, WV0 - WQ0:WV0 - WQ0 + LANES].T
        if ctx:
            kw32_ref[rows.start // sub] = kw.T
            vw32_ref[rows.start // sub] = vwt
        kw_ref[rows, :] = rot(kw, rows).astype(BF16)
        vwt_ref[:, rows] = vwt.astype(BF16)

    third = IN_WIDTH // 3
    rel = lambda c0: c0 - RET0

    def a_epilogue(pa, rows):
        glob_epilogue(pa[:, :RET0], rows)
        ret_store(pa[:, RET0:], rows, 0, rel(third))

    def b_epilogue(pb, rows):
        ret_store(pb[:, :LANES], rows, rel(third), 256)
        ret_store(pb[:, LANES:LANES + 256], rows, 256, 512, QK_SCALE)
        ret_store(pb[:, LANES + 256:], rows, 512, rel(2 * third))

    def c_epilogue(pc, rows):
        ret_store(pc[:, :WQ0 - 2 * third], rows, rel(2 * third), 1024)
        win_epilogue(pc[:, WQ0 - 2 * third:], rows)

    tiles = [slice(t * sub, (t + 1) * sub) for t in range(IN_SUBTILES)]
    hb = prologue(tiles[0])
    late = None
    for t, rows in enumerate(tiles):
        pa = _dot(hb, w_ref[:, 0:third])
        if late is not None:
            late()
        hb_next = prologue(tiles[t + 1]) if t + 1 < IN_SUBTILES else None
        pc = _dot(hb, w_ref[:, 2 * third:IN_WIDTH])
        a_epilogue(pa, rows)
        pb = _dot(hb, w_ref[:, third:2 * third])
        c_epilogue(pc, rows)
        late = functools.partial(b_epilogue, pb, rows)
        hb = hb_next
    late()


def _in_proj(x2d, mod, gpre, w_in_b, gq_t, gk_t, bd, rope_tabs, *, layer, ctx, rows_per_batch):
    t_rows = x2d.shape[0]
    tm = TM_IN
    tiles_per_batch = rows_per_batch // tm
    rope = rope_tabs is not None
    const2 = lambda i: (0, 0)
    lay3 = lambda i: (layer, 0, 0)
    row = lambda i: (i, 0)
    col = lambda i: (0, i)
    in_specs = [pl.BlockSpec((tm, D_MODEL), row),
                _mod_spec(layer, ctx, tiles_per_batch),
                pl.BlockSpec((None, 1, D_MODEL), lay3),
                _resident((D_MODEL, IN_WIDTH), const2),
                pl.BlockSpec((None, 1, LANES), lay3),
                pl.BlockSpec((None, 1, LANES), lay3),
                pl.BlockSpec((LANES, LANES), const2)]
    args = [x2d, mod, gpre, w_in_b, gq_t, gk_t, bd]
    if rope:
        tab = pl.BlockSpec((tm, LANES), lambda i: (i % tiles_per_batch, 0))
        in_specs += [tab, tab, tab]
        args += list(rope_tabs)
    bf = lambda shape: jax.ShapeDtypeStruct(shape, BF16)
    out_shape = [bf((t_rows, GLOB_HEADS * LANES)), bf((t_rows, LANES)), bf((LANES, t_rows)),
                 bf((t_rows, 4 * RET_HEADS * HEAD_DIM)),
                 bf((t_rows, WIN_HEADS * LANES)), bf((t_rows, LANES)), bf((LANES, t_rows))]
    out_specs = [pl.BlockSpec((tm, GLOB_HEADS * LANES), row), pl.BlockSpec((tm, LANES), row),
                 pl.BlockSpec((LANES, tm), col),
                 pl.BlockSpec((tm, 4 * RET_HEADS * HEAD_DIM), row),
                 pl.BlockSpec((tm, WIN_HEADS * LANES), row), pl.BlockSpec((tm, LANES), row),
                 pl.BlockSpec((LANES, tm), col)]
    if ctx:
        assert tm // IN_SUBTILES == rows_per_batch
        out_shape += [jax.ShapeDtypeStruct((t_rows // rows_per_batch, LANES, rows_per_batch), F32)] * 4
        out_specs += [pl.BlockSpec((IN_SUBTILES, LANES, rows_per_batch), lambda i: (i, 0, 0))] * 4
    return pl.pallas_call(
        functools.partial(_in_proj_kernel, rope=rope, ctx=ctx),
        grid=(t_rows // tm,),
        in_specs=in_specs, out_specs=out_specs, out_shape=out_shape,
        compiler_params=_cparams(("parallel",)),
        name="in_proj_ctx" if ctx else "in_proj_smp",
    )(*args)


def _attn_kernel(*refs, kinds, sink_layer, sub, n_lat, stack, cast):
    q_ref = refs[0]
    pos = 1
    srcs = []
    for kind in kinds:
        cnt = 6 if kind == "band" else 2
        srcs.append((kind, refs[pos:pos + cnt]))
        pos += cnt
    if sink_layer is not None:
        sink_ref = refs[pos]
        pos += 1
    if cast:
        refs[pos + 2][...] = refs[pos][...].astype(BF16)
        pos += 1
    o_ref = refs[pos]
    j = pl.program_id(1)
    nsub = q_ref.shape[0] // sub
    width = sub + 2 * WINDOW
    cols = stack * sub
    col_q = lax.broadcasted_iota(jnp.int32, (1, cols), 1)
    if "band" in kinds:
        krel = lax.broadcasted_iota(jnp.int32, (width, cols), 0) - WINDOW
        qrel = lax.broadcasted_iota(jnp.int32, (width, cols), 1) & (sub - 1)
        in_band = jnp.abs(krel - qrel) <= WINDOW

    def band_mask(bi):
        ok = in_band
        first = j * (nsub * sub) + bi * sub
        if bi == 0:
            ok = ok & (krel + first >= 0)
        if bi == nsub - 1:
            ok = ok & (krel + first < n_lat)
        return ok

    ones = lambda w: jnp.ones((ONES_ROWS, w), BF16)

    def key_rows(kind, r, bi):
        nk = r[0].shape[0]
        if kind == "split":
            return slice(bi * (nk // nsub), (bi + 1) * (nk // nsub))
        return slice(0, nk)

    band_k = {}
    band_vt = {}

    def keys(bi):
        out = []
        for kind, r in srcs:
            if kind == "band":
                if not band_k:
                    band_k[0] = jnp.concatenate([r[0][...], r[1][...], r[2][...]], axis=0)
                out.append(band_k[0][bi * sub:bi * sub + width, :])
            else:
                out.append(r[0][key_rows(kind, r, bi), :])
        return out

    def values(bi, kv):
        half = slice(HEAD_DIM * kv, HEAD_DIM * (kv + 1))
        out = []
        for kind, r in srcs:
            if kind == "band":
                if kv not in band_vt:
                    band_vt[kv] = jnp.concatenate([r[3][half, :], r[4][half, :], r[5][half, :]], axis=1)
                vt = band_vt[kv][:, bi * sub:bi * sub + width]
            else:
                vt = r[1][half, key_rows(kind, r, bi)]
            out.append(jnp.concatenate([vt, ones(vt.shape[1])], axis=0))
        return out

    def scores(bi, heads):
        rows = slice(bi * sub, (bi + 1) * sub)
        qs = [q_ref[rows, hd * LANES:(hd + 1) * LANES] for hd in heads]
        qs = qs[0] if len(qs) == 1 else jnp.concatenate(qs, axis=0)
        sts = []
        for (kind, _), k in zip(srcs, keys(bi)):
            st = _dot_nt(k, qs)
            sts.append(jnp.where(band_mask(bi), st, NEG) if kind == "band" else st)
        return sts

    def finish(bi, heads, sts):
        m = sts[0].max(axis=0, keepdims=True)
        for st in sts[1:]:
            m = jnp.maximum(m, st.max(axis=0, keepdims=True))
        if sink_layer is not None:
            sk = jnp.broadcast_to(sink_ref[sink_layer, heads[-1]] * LOG2E, (1, cols))
            for i in range(len(heads) - 2, -1, -1):
                sk = jnp.where(col_q < (i + 1) * sub, sink_ref[sink_layer, heads[i]] * LOG2E, sk)
            m = jnp.maximum(m, sk)
        ps = [jnp.exp2(st - m).astype(BF16) for st in sts]
        outs = []
        for kv in sorted({hd // GROUP for hd in heads}):
            mine = [i for i, hd in enumerate(heads) if hd // GROUP == kv]
            span = slice(mine[0] * sub, (mine[-1] + 1) * sub)
            acc = jnp.zeros((HEAD_DIM + ONES_ROWS, len(mine) * sub), F32)
            for p, vt in zip(ps, values(bi, kv)):
                acc = acc + _dot(vt, p[:, span])
            den = acc[HEAD_DIM:HEAD_DIM + 1, :]
            if sink_layer is not None:
                den = den + jnp.exp2(sk[:, span] - m[:, span])
            o = acc[:HEAD_DIM, :] / den
            outs += [o[:, i * sub:(i + 1) * sub] for i in range(len(mine))]
        return outs

    items = [(bi, list(range(g, g + stack))) for bi in range(nsub) for g in range(0, GLOB_HEADS, stack)]
    pieces = []
    pending = scores(*items[0])
    for i, (bi, heads) in enumerate(items):
        following = scores(*items[i + 1]) if i + 1 < len(items) else None
        pieces += finish(bi, heads, pending)
        pending = following

    for bi in range(nsub):
        for slab in range(GLOB_HEADS // 2):
            hd = bi * GLOB_HEADS + 2 * slab
            t = jnp.concatenate([pieces[hd], pieces[hd + 1]], axis=0)
            o_ref[bi * sub:(bi + 1) * sub, slab * LANES:(slab + 1) * LANES] = t.T.astype(BF16)


def _attention(q, sources, kinds, sink, sink_layer, *, rows, sub, stack, n, name, cast=None):
    t_rows = q.shape[0]
    nq = max(n // rows, 1)
    steps = t_rows // rows
    in_specs = [pl.BlockSpec((rows, q.shape[1]), lambda b, j: (b * nq + j, 0))]
    args = [q]
    for arr, spec in sources:
        in_specs.append(spec)
        args.append(arr)
    if sink is not None:
        in_specs.append(pl.BlockSpec(memory_space=pltpu.SMEM))
        args.append(sink)
    ow = GLOB_HEADS * HEAD_DIM
    out_specs = [pl.BlockSpec((rows, ow), lambda b, j: (b * nq + j, 0))]
    out_shape = [jax.ShapeDtypeStruct((t_rows, ow), BF16)]
    if cast is not None:
        w, w_layer = cast
        _, wr, wc = w.shape
        in_specs.append(pl.BlockSpec((None, wr // steps, wc), lambda b, j: (w_layer, b * nq + j, 0)))
        args.append(w)
        out_specs.append(pl.BlockSpec((wr // steps, wc), lambda b, j: (b * nq + j, 0)))
        out_shape.append(jax.ShapeDtypeStruct((wr, wc), BF16))
    outs = pl.pallas_call(
        functools.partial(_attn_kernel, kinds=kinds, sink_layer=sink_layer if sink is not None else None,
                          sub=sub, n_lat=n, stack=stack, cast=cast is not None),
        grid=(steps // nq, nq),
        in_specs=in_specs, out_specs=out_specs, out_shape=out_shape,
        compiler_params=_cparams(("parallel", "parallel")),
        name=name,
    )(*args)
    return outs if cast is not None else outs[0]


def _full_source(k, vt, n):
    return [(k, pl.BlockSpec((n, LANES), lambda b, j: (b, 0))),
            (vt, pl.BlockSpec((LANES, n), lambda b, j: (0, b)))]


def _cache_source(k_all, vt_all, layer, past):
    return [(k_all, pl.BlockSpec((None, past, LANES), lambda b, j: (layer, b, 0))),
            (vt_all, pl.BlockSpec((None, LANES, past), lambda b, j: (layer, 0, b)))]


def _band_source(k, vt, n, tq):
    per = n // WINDOW
    step = tq // WINDOW
    prev = lambda b, j: b * per + jnp.maximum(step * j - 1, 0)
    nxt = lambda b, j: b * per + jnp.minimum(step * j + step, per - 1)
    mid = lambda b, j: b * (n // tq) + j
    return [(k, pl.BlockSpec((WINDOW, LANES), lambda b, j: (prev(b, j), 0))),
            (k, pl.BlockSpec((tq, LANES), lambda b, j: (mid(b, j), 0))),
            (k, pl.BlockSpec((WINDOW, LANES), lambda b, j: (nxt(b, j), 0))),
            (vt, pl.BlockSpec((LANES, WINDOW), lambda b, j: (0, prev(b, j)))),
            (vt, pl.BlockSpec((LANES, tq), lambda b, j: (0, mid(b, j)))),
            (vt, pl.BlockSpec((LANES, WINDOW), lambda b, j: (0, nxt(b, j))))]


def _ret_kernel(*refs, n, chunk, has_init, emit_state, bpb):
    r_ref, dec_ref, bd_ref = refs[:3]
    pos = 3
    if has_init:
        s0f_ref, s0b_ref = refs[pos:pos + 2]
        pos += 2
    o_ref = refs[pos]
    pos += 1
    if emit_state:
        sf_ref, sb_ref = refs[pos:pos + 2]
        pos += 2
    stf, stb, dsum, dec, cdec = refs[pos:pos + 5]
    nc = n // chunk
    n_slab = RET_HEADS // 2
    use_inter = has_init or nc > 1
    lane_lo = lax.broadcasted_iota(jnp.int32, (chunk, LANES), 1) < HEAD_DIM
    row_s = lax.broadcasted_iota(jnp.int32, (LANES, LANES), 0) < HEAD_DIM
    lane_s = lax.broadcasted_iota(jnp.int32, (LANES, LANES), 1) < HEAD_DIM
    blockdiag = row_s == lane_s

    @pl.when(pl.program_id(0) == 0)
    def _():
        x = dec_ref[...]
        lg = jnp.minimum(x, 0.0) - jnp.log(1.0 + jnp.exp(-jnp.abs(x)))
        rowc = lax.broadcasted_iota(jnp.int32, (chunk, LANES), 0).astype(F32)
        diff = (lax.broadcasted_iota(jnp.int32, (chunk, chunk), 0)
                - lax.broadcasted_iota(jnp.int32, (chunk, chunk), 1)).astype(F32)
        for hd in range(RET_HEADS):
            lf = lg[hd:hd + 1, 0:1]
            lb = lg[RET_HEADS + hd:RET_HEADS + hd + 1, 0:1]
            dsum[hd] = (jnp.where(diff >= 0, jnp.exp(jnp.maximum(diff, 0.0) * lf), 0.0)
                        + jnp.where(diff <= 0, jnp.exp(jnp.maximum(-diff, 0.0) * lb), 0.0))
        for sl in range(n_slab):
            f0, f1 = lg[2 * sl:2 * sl + 1], lg[2 * sl + 1:2 * sl + 2]
            b0 = lg[RET_HEADS + 2 * sl:RET_HEADS + 2 * sl + 1]
            b1 = lg[RET_HEADS + 2 * sl + 1:RET_HEADS + 2 * sl + 2]
            lf2 = jnp.where(lane_lo, f0, f1)
            lb2 = jnp.where(lane_lo, b0, b1)
            dec[0 + sl] = jnp.exp((chunk - 1.0 - rowc) * lf2)
            dec[2 + sl] = jnp.exp(rowc * lb2)
            dec[4 + sl] = jnp.exp((rowc + 1.0) * lf2)
            dec[6 + sl] = jnp.exp((chunk - rowc) * lb2)
            cdec[sl] = jnp.exp(chunk * jnp.where(row_s, f0, f1))
            cdec[n_slab + sl] = jnp.exp(chunk * jnp.where(row_s, b0, b1))

    lo, hi = slice(0, HEAD_DIM), slice(HEAD_DIM, LANES)
    for bi in range(bpb):
        for sl in range(n_slab):
            stf[bi, 0, sl] = jnp.zeros((LANES, LANES), F32)
            stb[bi, nc, sl] = jnp.zeros((LANES, LANES), F32)
            if has_init:
                stf[bi, 0, sl, lo, lo] = s0f_ref[bi, 2 * sl]
                stf[bi, 0, sl, hi, hi] = s0f_ref[bi, 2 * sl + 1]
                stb[bi, nc, sl, lo, lo] = s0b_ref[bi, 2 * sl]
                stb[bi, nc, sl, hi, hi] = s0b_ref[bi, 2 * sl + 1]

    def state_chain(bi, ci, sl):
        rows = _chunk_rows(bi * n, ci, chunk)
        yield
        k2 = r_ref[rows, 256 + sl * LANES:256 + (sl + 1) * LANES].astype(F32)
        kk = jnp.concatenate([(k2 * dec[0 + sl]).astype(BF16), (k2 * dec[2 + sl]).astype(BF16)], axis=1)
        yield
        u = _dot_tn(kk, r_ref[rows, 512 + sl * LANES:512 + (sl + 1) * LANES])
        yield
        stf[bi, ci + 1, sl] = jnp.where(blockdiag, u[:LANES], 0.0)
        stb[bi, ci, sl] = jnp.where(blockdiag, u[LANES:], 0.0)

    def output_chain(bi, ci, sl):
        rows = _chunk_rows(bi * n, ci, chunk)
        cols = lambda off: slice(off + sl * LANES, off + (sl + 1) * LANES)
        q2 = r_ref[rows, cols(0)]
        k2 = r_ref[rows, cols(256)]
        zero = jnp.zeros_like(q2)
        s_lo = _dot_nt(jnp.where(lane_lo, q2, zero), k2)
        s_hi = _dot_nt(jnp.where(lane_lo, zero, q2), k2)
        if use_inter:
            it = _dot(q2, jnp.concatenate([stf[bi, ci, sl], stb[bi, ci + 1, sl]], axis=1).astype(BF16))
        yield
        a_lo = (s_lo * dsum[2 * sl]).astype(BF16)
        a_hi = (s_hi * dsum[2 * sl + 1]).astype(BF16)
        yield
        v2 = r_ref[rows, cols(512)]
        o_lo, o_hi = _dot(a_lo, v2), _dot(a_hi, v2)
        yield
        o2 = jnp.where(lane_lo, o_lo, o_hi)
        if use_inter:
            o2 = o2 + it[:, :LANES] * dec[4 + sl] + it[:, LANES:] * dec[6 + sl]
        sq = (o2 * o2).astype(BF16)
        yield
        ss = _dot(sq, bd_ref[...])
        yield
        g2 = r_ref[rows, cols(768)].astype(F32)
        o_ref[rows, sl * LANES:(sl + 1) * LANES] = (o2 * lax.rsqrt(ss + EPS) * _silu(g2)).astype(BF16)

    def scan_states(bi):
        for sl in range(n_slab):
            for ci in range(nc):
                stf[bi, ci + 1, sl] = stf[bi, ci, sl] * cdec[sl] + stf[bi, ci + 1, sl]
            for ci in range(nc - 1, -1, -1):
                stb[bi, ci, sl] = stb[bi, ci + 1, sl] * cdec[n_slab + sl] + stb[bi, ci, sl]
            if emit_state:
                sf_ref[bi, 2 * sl] = stf[bi, nc, sl, lo, lo]
                sf_ref[bi, 2 * sl + 1] = stf[bi, nc, sl, hi, hi]
                sb_ref[bi, 2 * sl] = stb[bi, 0, sl, lo, lo]
                sb_ref[bi, 2 * sl + 1] = stb[bi, 0, sl, hi, hi]

    slabs = range(n_slab)
    if not use_inter:
        _run_skewed([state_chain(bi, 0, sl) for bi in range(bpb) for sl in slabs]
                    + [output_chain(bi, 0, sl) for bi in range(bpb) for sl in slabs])
        for bi in range(bpb):
            scan_states(bi)
    else:
        for bi in range(bpb):
            group = RET_CHUNK_GROUP if nc % RET_CHUNK_GROUP == 0 else 1

            def local_state(i, carry):
                _run_skewed([state_chain(bi, i * group + u, sl) for u in range(group) for sl in slabs])
                return carry

            def outputs(i, carry):
                _run_skewed([output_chain(bi, i * group + u, sl) for u in range(group) for sl in slabs])
                return carry

            lax.fori_loop(0, nc // group, local_state, 0)
            scan_states(bi)
            lax.fori_loop(0, nc // group, outputs, 0)


def _retention(r, dec_tab, bd, init, *, layer, bsz, n, emit_state, name, bpb=1):
    rw = r.shape[1]
    chunk = RET_CHUNK
    nc = n // chunk
    n_slab = RET_HEADS // 2
    has_init = init is not None
    in_specs = [pl.BlockSpec((bpb * n, rw), lambda b: (b, 0)),
                pl.BlockSpec((None, 8, LANES), lambda b: (layer, 0, 0)),
                pl.BlockSpec((LANES, LANES), lambda b: (0, 0))]
    args = [r, dec_tab, bd]
    if has_init:
        init_spec = pl.BlockSpec((bpb, None, RET_HEADS, HEAD_DIM, HEAD_DIM), lambda b: (b, layer, 0, 0, 0))
        in_specs += [init_spec, init_spec]
        args += list(init)
    ow = RET_HEADS * HEAD_DIM
    out_shape = [jax.ShapeDtypeStruct((bsz * n, ow), BF16)]
    out_specs = [pl.BlockSpec((bpb * n, ow), lambda b: (b, 0))]
    if emit_state:
        st_spec = pl.BlockSpec((bpb, RET_HEADS, HEAD_DIM, HEAD_DIM), lambda b: (b, 0, 0, 0))
        out_shape += [jax.ShapeDtypeStruct((bsz, RET_HEADS, HEAD_DIM, HEAD_DIM), F32)] * 2
        out_specs += [st_spec, st_spec]
    return pl.pallas_call(
        functools.partial(_ret_kernel, n=n, chunk=chunk, has_init=has_init, emit_state=emit_state, bpb=bpb),
        grid=(bsz // bpb,),
        in_specs=in_specs, out_specs=out_specs, out_shape=out_shape,
        scratch_shapes=[pltpu.VMEM((bpb, nc + 1, n_slab, LANES, LANES), F32),
                        pltpu.VMEM((bpb, nc + 1, n_slab, LANES, LANES), F32),
                        pltpu.VMEM((RET_HEADS, chunk, chunk), F32),
                        pltpu.VMEM((8, chunk, LANES), F32),
                        pltpu.VMEM((2 * n_slab, LANES, LANES), F32)],
        compiler_params=_cparams(("arbitrary",)),
        name=name,
    )(*args)


def _post_kernel(x_ref, og_ref, or_ref, ow_ref, mod_ref, wo_ref, wgu_ref, wd_ref,
                 gpost_ref, gpre2_ref, gpost2_ref, o_ref):
    m = mod_ref[0]
    gt1 = m[:, 2 * D_MODEL:3 * D_MODEL]
    sh2 = m[:, 3 * D_MODEL:4 * D_MODEL]
    sc2 = m[:, 4 * D_MODEL:5 * D_MODEL]
    gt2 = m[:, 5 * D_MODEL:6 * D_MODEL]
    sub = x_ref.shape[0] // POST_SUBTILES
    tiles = [slice(t * sub, (t + 1) * sub) for t in range(POST_SUBTILES)]

    def out_proj(rows):
        merged = jnp.concatenate([og_ref[rows, :], or_ref[rows, :], ow_ref[rows, :]], axis=1)
        return _dot(merged, wo_ref[...])

    def norms(rows, mixed):
        x1 = x_ref[rows, :] + gt1 * (_row_rms(mixed) * gpost_ref[...])
        return x1, (_row_rms(x1) * (gpre2_ref[...] * (1.0 + sc2)) + sh2).astype(BF16)

    def ffn(h2):
        ff = None
        for lo, hi in FF_CHUNKS:
            act = (_silu(_dot(h2, wgu_ref[:, lo:hi])) * _dot(h2, wgu_ref[:, D_FF + lo:D_FF + hi])).astype(BF16)
            part = _dot(act, wd_ref[lo:hi, :])
            ff = part if ff is None else ff + part
        return ff

    mixed = [out_proj(rows) for rows in tiles]
    x1s, ffs = [], []
    for rows, mx in zip(tiles, mixed):
        x1, h2 = norms(rows, mx)
        x1s.append(x1)
        ffs.append(ffn(h2))
    for rows, x1, ff in zip(tiles, x1s, ffs):
        o_ref[rows, :] = x1 + gt2 * (_row_rms(ff) * gpost2_ref[...])


def _post(x2d, og, orr, ow, mod, w_out_b, w_gu_b, w_down_b, gpost, gpre2, gpost2, *, layer, ctx, rows_per_batch):
    t_rows = x2d.shape[0]
    tm = TM_POST
    row = lambda i: (i, 0)
    lay3 = lambda i: (layer, 0, 0)
    gain = pl.BlockSpec((None, 1, D_MODEL), lay3)
    return pl.pallas_call(
        _post_kernel,
        grid=(t_rows // tm,),
        in_specs=[pl.BlockSpec((tm, D_MODEL), row),
                  pl.BlockSpec((tm, og.shape[1]), row),
                  pl.BlockSpec((tm, orr.shape[1]), row),
                  pl.BlockSpec((tm, ow.shape[1]), row),
                  _mod_spec(layer, ctx, rows_per_batch // tm),
                  _resident((D_MODEL, D_MODEL), lambda i: (0, 0)),
                  _resident((D_MODEL, 2 * D_FF), lambda i: (0, 0)),
                  _resident((D_FF, D_MODEL), lambda i: (0, 0)),
                  gain, gain, gain],
        out_specs=pl.BlockSpec((tm, D_MODEL), row),
        out_shape=jax.ShapeDtypeStruct((t_rows, D_MODEL), F32),
        compiler_params=_cparams(("parallel",)),
        name="post_mixer",
    )(x2d, og, orr, ow, mod, w_out_b, w_gu_b, w_down_b, gpost, gpre2, gpost2)


def _rope_tables(n):
    half = HEAD_DIM // 4
    pos = jnp.arange(n, dtype=jnp.int32)
    row = (pos // GRID_W).astype(F32)
    col = (pos % GRID_W).astype(F32)
    freqs = ROPE_BASE ** (-jnp.arange(half, dtype=F32) / half)
    ang_r = row[:, None] * freqs[None, :]
    ang_c = col[:, None] * freqs[None, :]
    ang = jnp.concatenate([ang_r, ang_r, ang_c, ang_c], axis=-1)
    second = (jnp.arange(HEAD_DIM) % (2 * half)) >= half
    cos = jnp.cos(ang)
    sin = jnp.sin(ang)
    sin_a = jnp.where(second[None, :], sin, 0.0)
    sin_b = jnp.where(second[None, :], 0.0, -sin)
    tile = lambda t: jnp.concatenate([t, t], axis=-1)
    return tile(cos), tile(sin_a), tile(sin_b)


def _cache_layouts(ck, cv):
    b, _, past = ck.shape[:3]
    k = jnp.transpose(ck.reshape(b, DEPTH, past, LANES), (1, 0, 2, 3)).reshape(DEPTH, b * past, LANES)
    vt = jnp.transpose(cv.reshape(b, DEPTH, past, LANES), (1, 3, 0, 2)).reshape(DEPTH, LANES, b * past)
    return k.astype(BF16), vt.astype(BF16)


def kernel(x_prompt, x_sample, cache_glob_k, cache_glob_v, state_ret_fwd, state_ret_bwd, cache_win_k, cache_win_v, c, c_ctx, w_mod, b_mod, g_pre_mix, g_post_mix, g_pre_ffn, g_post_ffn, w_in, g_q, g_k, ret_decay_fwd, ret_decay_bwd, win_sink, w_out, w_gate_up, w_down):
    bc, nc_, _ = x_prompt.shape
    bs, ns, _ = x_sample.shape
    past = cache_glob_k.shape[2]

    cond8 = jnp.concatenate([c, c_ctx[None, :], jnp.zeros((MOD_ROWS - bs - 1, D_MODEL), F32)], axis=0)
    mod = _modulation(cond8, w_mod, b_mod).reshape(DEPTH * MOD_ROWS, 1, 6 * D_MODEL)

    w_in_b, w_out_b, w_gu_b, w_down_b = (w[0].astype(BF16) for w in (w_in, w_out, w_gate_up, w_down))
    rope_tabs = _rope_tables(ns)
    seg = jnp.arange(LANES) // HEAD_DIM
    bd = jnp.where(seg[:, None] == seg[None, :], 1.0 / HEAD_DIM, 0.0).astype(BF16)
    gain3 = lambda g: g.reshape(DEPTH, 1, D_MODEL)
    gpre, gpost, gpre2, gpost2 = gain3(g_pre_mix), gain3(g_post_mix), gain3(g_pre_ffn), gain3(g_post_ffn)
    gq_t = jnp.tile(g_q, (1, 2)).reshape(DEPTH, 1, LANES)
    gk_t = jnp.tile(g_k, (1, 2)).reshape(DEPTH, 1, LANES)
    dec_tab = jnp.broadcast_to(jnp.concatenate([ret_decay_fwd, ret_decay_bwd], axis=1)[:, :, None],
                               (DEPTH, 2 * RET_HEADS, LANES))
    init = (state_ret_fwd, state_ret_bwd)
    cgk, cgvt = _cache_layouts(cache_glob_k, cache_glob_v)
    cwk, cwvt = _cache_layouts(cache_win_k, cache_win_v)

    y = x_prompt.reshape(bc * nc_, D_MODEL)
    z = x_sample.reshape(bs * ns, D_MODEL)
    new = [[] for _ in range(6)]
    for l in range(DEPTH):
        qg, kg, vgt, r, qw, kw, vwt, kg32, vg32, kw32, vw32 = _in_proj(
            y, mod, gpre, w_in_b, gq_t, gk_t, bd, None, layer=l, ctx=True, rows_per_batch=nc_)
        nxt = l + 1 if l + 1 < DEPTH else None
        cast = lambda w: None if nxt is None else (w, nxt)
        og = _attention(qg, _full_source(kg, vgt, nc_ * CTX_BPB), ("split",), None, l, cast=cast(w_in),
                        rows=nc_ * CTX_BPB, sub=nc_, stack=GROUP, n=nc_, name="attn_glob_ctx")
        ow = _attention(qw, _full_source(kw, vwt, nc_ * CTX_BPB), ("split",), win_sink, l, cast=cast(w_out),
                        rows=nc_ * CTX_BPB, sub=nc_, stack=GROUP, n=nc_, name="attn_win_ctx")
        if nxt is not None:
            (og, w_in_next), (ow, w_out_next) = og, ow
        orr, sf, sb = _retention(r, dec_tab, bd, None, layer=l, bsz=bc, n=nc_, emit_state=True, bpb=CTX_BPB,
                                 name="ret_ctx")
        y = _post(y, og, orr, ow, mod, w_out_b, w_gu_b, w_down_b, gpost, gpre2, gpost2,
                  layer=l, ctx=True, rows_per_batch=nc_)
        for lst, t in zip(new, (kg32, vg32, sf, sb, kw32, vw32)):
            lst.append(t)

        qg, kg, vgt, r, qw, kw, vwt = _in_proj(
            z, mod, gpre, w_in_b, gq_t, gk_t, bd, rope_tabs, layer=l, ctx=False, rows_per_batch=ns)
        og = _attention(qg, _full_source(kg, vgt, ns) + _cache_source(cgk, cgvt, l, past),
                        ("shared", "shared"), None, l, cast=cast(w_gate_up),
                        rows=TQ_GLOB, sub=TQ_GLOB, stack=1, n=ns, name="attn_glob_smp")
        ow = _attention(qw, _band_source(kw, vwt, ns, TQ_WIN) + _cache_source(cwk, cwvt, l, past),
                        ("band", "shared"), win_sink, l, cast=cast(w_down),
                        rows=TQ_WIN, sub=WINDOW, stack=GLOB_HEADS, n=ns, name="attn_win_smp")
        if nxt is not None:
            (og, w_gu_next), (ow, w_down_next) = og, ow
        (orr,) = _retention(r, dec_tab, bd, init, layer=l, bsz=bs, n=ns, emit_state=False, name="ret_smp")
        z = _post(z, og, orr, ow, mod, w_out_b, w_gu_b, w_down_b, gpost, gpre2, gpost2,
                  layer=l, ctx=False, rows_per_batch=ns)
        if nxt is not None:
            w_in_b, w_out_b, w_gu_b, w_down_b = w_in_next, w_out_next, w_gu_next, w_down_next

    kv5 = lambda ts: jnp.transpose(jnp.stack(ts, axis=1).reshape(bc, DEPTH, KV_HEADS, HEAD_DIM, nc_),
                                   (0, 1, 4, 2, 3))
    st5 = lambda ts: jnp.stack(ts, axis=1)
    return (y.reshape(bc, nc_, D_MODEL), z.reshape(bs, ns, D_MODEL),
            kv5(new[0]), kv5(new[1]), st5(new[2]), st5(new[3]), kv5(new[4]), kv5(new[5]))
```

```python
import functools

import jax
import jax.numpy as jnp
from jax import lax
from jax.experimental import pallas as pl
from jax.experimental.pallas import tpu as pltpu

F32 = jnp.float32
BF16 = jnp.bfloat16

D_MODEL = 1024
DEPTH = 4
GRID_W = 64
HEAD_DIM = 64
LANES = 128
GLOB_HEADS = 6
RET_HEADS = 4
WIN_HEADS = 6
KV_HEADS = 2
GROUP = GLOB_HEADS // KV_HEADS
WINDOW = 128
ROPE_BASE = 10000.0
QK_SCALE = HEAD_DIM ** -0.5
LOG2E = 1.4426950408889634
D_FF = 2816
IN_WIDTH = 2304
EPS = 1e-6
NEG = -1e30
MOD_ROWS = 8
CTX_ROW = 4

GQ0, GK0, GV0 = 0, 384, 512
RET0 = 640
WQ0, WK0, WV0 = 1664, 2048, 2176

VMEM_LIMIT = 56 * 1024 * 1024

TM_IN = 1024
IN_SUBTILES = 4
TM_POST = 1024
POST_SUBTILES = 4
FF_CHUNKS = ((0, 1024), (1024, 2048), (2048, D_FF))
RET_CHUNK = 256
RET_CHUNK_GROUP = 4
CTX_BPB = 8
TQ_GLOB = 1024
TQ_WIN = 2048
ONES_ROWS = 16


def _cparams(sem):
    return pltpu.CompilerParams(dimension_semantics=sem, vmem_limit_bytes=VMEM_LIMIT)


def _resident(block_shape, index_map):
    return pl.BlockSpec(block_shape, index_map, pipeline_mode=pl.Buffered(1))


def _dot(a, b):
    return jnp.dot(a, b, preferred_element_type=F32)


def _dot_nt(a, b):
    return lax.dot_general(a, b, (((1,), (1,)), ((), ())), preferred_element_type=F32)


def _dot_tn(a, b):
    return lax.dot_general(a, b, (((0,), (0,)), ((), ())), preferred_element_type=F32)


def _silu(x):
    return x / (1.0 + jnp.exp(-x))


def _row_rms(x):
    return x * lax.rsqrt(jnp.mean(x * x, axis=-1, keepdims=True) + EPS)


def _chunk_rows(offset, ci, chunk):
    if isinstance(ci, int):
        return pl.ds(offset + ci * chunk, chunk)
    return pl.ds(pl.multiple_of(offset + ci * chunk, chunk), chunk)


def _run_skewed(chains):
    chains = list(chains)
    done = [False] * len(chains)
    t = 0
    while not all(done):
        live = [c for c in range(min(t + 1, len(chains))) if not done[c]]
        for c in sorted(live, key=lambda c: (t - c) % 2):
            try:
                next(chains[c])
            except StopIteration:
                done[c] = True
        t += 1


def _mod_kernel(c_ref, w_ref, b_ref, o_ref):
    s = _silu(c_ref[...]).astype(BF16)
    o_ref[0] = _dot(s, w_ref[0].astype(BF16)) + b_ref[0]


def _modulation(cond8, w_mod, b_mod):
    tn = 1536
    n_out = 6 * D_MODEL
    return pl.pallas_call(
        _mod_kernel,
        grid=(DEPTH, n_out // tn),
        in_specs=[pl.BlockSpec((MOD_ROWS, D_MODEL), lambda l, j: (0, 0)),
                  pl.BlockSpec((1, D_MODEL, tn), lambda l, j: (l, 0, j)),
                  pl.BlockSpec((1, 1, tn), lambda l, j: (l, 0, j))],
        out_specs=pl.BlockSpec((1, MOD_ROWS, tn), lambda l, j: (l, 0, j)),
        out_shape=jax.ShapeDtypeStruct((DEPTH, MOD_ROWS, n_out), F32),
        compiler_params=_cparams(("parallel", "parallel")),
        name="modulation",
    )(cond8, w_mod, b_mod.reshape(DEPTH, 1, n_out))


def _mod_spec(layer, ctx, tiles_per_batch):
    if ctx:
        return pl.BlockSpec((1, 1, 6 * D_MODEL), lambda i: (layer * MOD_ROWS + CTX_ROW, 0, 0))
    return pl.BlockSpec((1, 1, 6 * D_MODEL), lambda i: (layer * MOD_ROWS + i // tiles_per_batch, 0, 0))


def _in_proj_kernel(*refs, rope, ctx):
    x_ref, mod_ref, gpre_ref, w_ref, gq_ref, gk_ref, bd_ref = refs[:7]
    pos = 7
    if rope:
        cos_ref, sa_ref, sb_ref = refs[pos:pos + 3]
        pos += 3
    qg_ref, kg_ref, vgt_ref, r_ref, qw_ref, kw_ref, vwt_ref = refs[pos:pos + 7]
    pos += 7
    if ctx:
        kg32_ref, vg32_ref, kw32_ref, vw32_ref = refs[pos:pos + 4]

    m = mod_ref[0]
    sh1 = m[:, 0:D_MODEL]
    sc1 = m[:, D_MODEL:2 * D_MODEL]
    sub = x_ref.shape[0] // IN_SUBTILES
    lane_lo = lax.broadcasted_iota(jnp.int32, (sub, LANES), 1) < HEAD_DIM

    def prologue(rows):
        return (_row_rms(x_ref[rows, :]) * (gpre_ref[...] * (1.0 + sc1)) + sh1).astype(BF16)

    def head_norm(t, g):
        ss = _dot((t * t).astype(BF16), bd_ref[...])
        return t * lax.rsqrt(ss + EPS) * g

    def rot(t, rows):
        if not rope:
            return t
        return (t * cos_ref[rows, :] + pltpu.roll(t, 16, 1) * sa_ref[rows, :]
                + pltpu.roll(t, LANES - 16, 1) * sb_ref[rows, :])

    def store_q(slabs, q_ref, rows):
        for hd in range(GLOB_HEADS):
            t = slabs[hd // 2]
            kv = hd // GROUP
            if hd % 2 != kv:
                t = pltpu.roll(t, HEAD_DIM, 1)
            keep = lane_lo if kv == 0 else jnp.logical_not(lane_lo)
            q_ref[rows, hd * LANES:(hd + 1) * LANES] = jnp.where(keep, t * (QK_SCALE * LOG2E), 0.0).astype(BF16)

    def glob_epilogue(pg, rows):
        store_q([rot(head_norm(pg[:, s * LANES:(s + 1) * LANES], gq_ref[...]), rows) for s in range(3)],
                qg_ref, rows)
        kg = head_norm(pg[:, GK0:GK0 + LANES], gk_ref[...])
        vgt = pg[:, GV0:GV0 + LANES].T
        if ctx:
            kg32_ref[rows.start // sub] = kg.T
            vg32_ref[rows.start // sub] = vgt
        kg_ref[rows, :] = rot(kg, rows).astype(BF16)
        vgt_ref[:, rows] = vgt.astype(BF16)

    def ret_store(p, rows, lo, hi, scale=None):
        r_ref[rows, lo:hi] = (p if scale is None else p * scale).astype(BF16)

    def win_epilogue(pw, rows):
        store_q([rot(pw[:, s * LANES:(s + 1) * LANES], rows) for s in range(3)], qw_ref, rows)
        kw = pw[:, WK0 - WQ0:WK0 - WQ0 + LANES]
        vwt = pw[:, WV0 - WQ0:WV0 - WQ0 + LANES].T
        if ctx:
            kw32_ref[rows.start // sub] = kw.T
            vw32_ref[rows.start // sub] = vwt
        kw_ref[rows, :] = rot(kw, rows).astype(BF16)
        vwt_ref[:, rows] = vwt.astype(BF16)

    third = IN_WIDTH // 3
    rel = lambda c0: c0 - RET0

    def a_epilogue(pa, rows):
        glob_epilogue(pa[:, :RET0], rows)
        ret_store(pa[:, RET0:], rows, 0, rel(third))

    def b_epilogue(pb, rows):
        ret_store(pb[:, :LANES], rows, rel(third), 256)
        ret_store(pb[:, LANES:LANES + 256], rows, 256, 512, QK_SCALE)
        ret_store(pb[:, LANES + 256:], rows, 512, rel(2 * third))

    def c_epilogue(pc, rows):
        ret_store(pc[:, :WQ0 - 2 * third], rows, rel(2 * third), 1024)
        win_epilogue(pc[:, WQ0 - 2 * third:], rows)

    tiles = [slice(t * sub, (t + 1) * sub) for t in range(IN_SUBTILES)]
    hb = prologue(tiles[0])
    late = None
    for t, rows in enumerate(tiles):
        pa = _dot(hb, w_ref[:, 0:third])
        if late is not None:
            late()
        hb_next = prologue(tiles[t + 1]) if t + 1 < IN_SUBTILES else None
        pc = _dot(hb, w_ref[:, 2 * third:IN_WIDTH])
        a_epilogue(pa, rows)
        pb = _dot(hb, w_ref[:, third:2 * third])
        c_epilogue(pc, rows)
        late = functools.partial(b_epilogue, pb, rows)
        hb = hb_next
    late()


def _in_proj(x2d, mod, gpre, w_in_b, gq_t, gk_t, bd, rope_tabs, *, layer, ctx, rows_per_batch):
    t_rows = x2d.shape[0]
    tm = TM_IN
    tiles_per_batch = rows_per_batch // tm
    rope = rope_tabs is not None
    const2 = lambda i: (0, 0)
    lay3 = lambda i: (layer, 0, 0)
    row = lambda i: (i, 0)
    col = lambda i: (0, i)
    in_specs = [pl.BlockSpec((tm, D_MODEL), row),
                _mod_spec(layer, ctx, tiles_per_batch),
                pl.BlockSpec((None, 1, D_MODEL), lay3),
                _resident((D_MODEL, IN_WIDTH), const2),
                pl.BlockSpec((None, 1, LANES), lay3),
                pl.BlockSpec((None, 1, LANES), lay3),
                pl.BlockSpec((LANES, LANES), const2)]
    args = [x2d, mod, gpre, w_in_b, gq_t, gk_t, bd]
    if rope:
        tab = pl.BlockSpec((tm, LANES), lambda i: (i % tiles_per_batch, 0))
        in_specs += [tab, tab, tab]
        args += list(rope_tabs)
    bf = lambda shape: jax.ShapeDtypeStruct(shape, BF16)
    out_shape = [bf((t_rows, GLOB_HEADS * LANES)), bf((t_rows, LANES)), bf((LANES, t_rows)),
                 bf((t_rows, 4 * RET_HEADS * HEAD_DIM)),
                 bf((t_rows, WIN_HEADS * LANES)), bf((t_rows, LANES)), bf((LANES, t_rows))]
    out_specs = [pl.BlockSpec((tm, GLOB_HEADS * LANES), row), pl.BlockSpec((tm, LANES), row),
                 pl.BlockSpec((LANES, tm), col),
                 pl.BlockSpec((tm, 4 * RET_HEADS * HEAD_DIM), row),
                 pl.BlockSpec((tm, WIN_HEADS * LANES), row), pl.BlockSpec((tm, LANES), row),
                 pl.BlockSpec((LANES, tm), col)]
    if ctx:
        assert tm // IN_SUBTILES == rows_per_batch
        out_shape += [jax.ShapeDtypeStruct((t_rows // rows_per_batch, LANES, rows_per_batch), F32)] * 4
        out_specs += [pl.BlockSpec((IN_SUBTILES, LANES, rows_per_batch), lambda i: (i, 0, 0))] * 4
    return pl.pallas_call(
        functools.partial(_in_proj_kernel, rope=rope, ctx=ctx),
        grid=(t_rows // tm,),
        in_specs=in_specs, out_specs=out_specs, out_shape=out_shape,
        compiler_params=_cparams(("parallel",)),
        name="in_proj_ctx" if ctx else "in_proj_smp",
    )(*args)


def _attn_kernel(*refs, kinds, sink_layer, sub, n_lat, stack, cast):
    q_ref = refs[0]
    pos = 1
    srcs = []
    for kind in kinds:
        cnt = 6 if kind == "band" else 2
        srcs.append((kind, refs[pos:pos + cnt]))
        pos += cnt
    if sink_layer is not None:
        sink_ref = refs[pos]
        pos += 1
    if cast:
        refs[pos + 2][...] = refs[pos][...].astype(BF16)
        pos += 1
    o_ref = refs[pos]
    j = pl.program_id(1)
    nsub = q_ref.shape[0] // sub
    width = sub + 2 * WINDOW
    cols = stack * sub
    col_q = lax.broadcasted_iota(jnp.int32, (1, cols), 1)
    if "band" in kinds:
        krel = lax.broadcasted_iota(jnp.int32, (width, cols), 0) - WINDOW
        qrel = lax.broadcasted_iota(jnp.int32, (width, cols), 1) & (sub - 1)
        in_band = jnp.abs(krel - qrel) <= WINDOW

    def band_mask(bi):
        ok = in_band
        first = j * (nsub * sub) + bi * sub
        if bi == 0:
            ok = ok & (krel + first >= 0)
        if bi == nsub - 1:
            ok = ok & (krel + first < n_lat)
        return ok

    ones = lambda w: jnp.ones((ONES_ROWS, w), BF16)

    def key_rows(kind, r, bi):
        nk = r[0].shape[0]
        if kind == "split":
            return slice(bi * (nk // nsub), (bi + 1) * (nk // nsub))
        return slice(0, nk)

    band_k = {}
    band_vt = {}

    def keys(bi):
        out = []
        for kind, r in srcs:
            if kind == "band":
                if not band_k:
                    band_k[0] = jnp.concatenate([r[0][...], r[1][...], r[2][...]], axis=0)
                out.append(band_k[0][bi * sub:bi * sub + width, :])
            else:
                out.append(r[0][key_rows(kind, r, bi), :])
        return out

    def values(bi, kv):
        half = slice(HEAD_DIM * kv, HEAD_DIM * (kv + 1))
        out = []
        for kind, r in srcs:
            if kind == "band":
                if kv not in band_vt:
                    band_vt[kv] = jnp.concatenate([r[3][half, :], r[4][half, :], r[5][half, :]], axis=1)
                vt = band_vt[kv][:, bi * sub:bi * sub + width]
            else:
                vt = r[1][half, key_rows(kind, r, bi)]
            out.append(jnp.concatenate([vt, ones(vt.shape[1])], axis=0))
        return out

    def scores(bi, heads):
        rows = slice(bi * sub, (bi + 1) * sub)
        qs = [q_ref[rows, hd * LANES:(hd + 1) * LANES] for hd in heads]
        qs = qs[0] if len(qs) == 1 else jnp.concatenate(qs, axis=0)
        sts = []
        for (kind, _), k in zip(srcs, keys(bi)):
            st = _dot_nt(k, qs)
            sts.append(jnp.where(band_mask(bi), st, NEG) if kind == "band" else st)
        return sts

    def finish(bi, heads, sts):
        m = sts[0].max(axis=0, keepdims=True)
        for st in sts[1:]:
            m = jnp.maximum(m, st.max(axis=0, keepdims=True))
        if sink_layer is not None:
            sk = jnp.broadcast_to(sink_ref[sink_layer, heads[-1]] * LOG2E, (1, cols))
            for i in range(len(heads) - 2, -1, -1):
                sk = jnp.where(col_q < (i + 1) * sub, sink_ref[sink_layer, heads[i]] * LOG2E, sk)
            m = jnp.maximum(m, sk)
        ps = [jnp.exp2(st - m).astype(BF16) for st in sts]
        outs = []
        for kv in sorted({hd // GROUP for hd in heads}):
            mine = [i for i, hd in enumerate(heads) if hd // GROUP == kv]
            span = slice(mine[0] * sub, (mine[-1] + 1) * sub)
            acc = jnp.zeros((HEAD_DIM + ONES_ROWS, len(mine) * sub), F32)
            for p, vt in zip(ps, values(bi, kv)):
                acc = acc + _dot(vt, p[:, span])
            den = acc[HEAD_DIM:HEAD_DIM + 1, :]
            if sink_layer is not None:
                den = den + jnp.exp2(sk[:, span] - m[:, span])
            o = acc[:HEAD_DIM, :] / den
            outs += [o[:, i * sub:(i + 1) * sub] for i in range(len(mine))]
        return outs

    items = [(bi, list(range(g, g + stack))) for bi in range(nsub) for g in range(0, GLOB_HEADS, stack)]
    pieces = []
    pending = scores(*items[0])
    for i, (bi, heads) in enumerate(items):
        following = scores(*items[i + 1]) if i + 1 < len(items) else None
        pieces += finish(bi, heads, pending)
        pending = following

    for bi in range(nsub):
        for slab in range(GLOB_HEADS // 2):
            hd = bi * GLOB_HEADS + 2 * slab
            t = jnp.concatenate([pieces[hd], pieces[hd + 1]], axis=0)
            o_ref[bi * sub:(bi + 1) * sub, slab * LANES:(slab + 1) * LANES] = t.T.astype(BF16)


def _attention(q, sources, kinds, sink, sink_layer, *, rows, sub, stack, n, name, cast=None):
    t_rows = q.shape[0]
    nq = max(n // rows, 1)
    steps = t_rows // rows
    in_specs = [pl.BlockSpec((rows, q.shape[1]), lambda b, j: (b * nq + j, 0))]
    args = [q]
    for arr, spec in sources:
        in_specs.append(spec)
        args.append(arr)
    if sink is not None:
        in_specs.append(pl.BlockSpec(memory_space=pltpu.SMEM))
        args.append(sink)
    ow = GLOB_HEADS * HEAD_DIM
    out_specs = [pl.BlockSpec((rows, ow), lambda b, j: (b * nq + j, 0))]
    out_shape = [jax.ShapeDtypeStruct((t_rows, ow), BF16)]
    if cast is not None:
        w, w_layer = cast
        _, wr, wc = w.shape
        in_specs.append(pl.BlockSpec((None, wr // steps, wc), lambda b, j: (w_layer, b * nq + j, 0)))
        args.append(w)
        out_specs.append(pl.BlockSpec((wr // steps, wc), lambda b, j: (b * nq + j, 0)))
        out_shape.append(jax.ShapeDtypeStruct((wr, wc), BF16))
    outs = pl.pallas_call(
        functools.partial(_attn_kernel, kinds=kinds, sink_layer=sink_layer if sink is not None else None,
                          sub=sub, n_lat=n, stack=stack, cast=cast is not None),
        grid=(steps // nq, nq),
        in_specs=in_specs, out_specs=out_specs, out_shape=out_shape,
        compiler_params=_cparams(("parallel", "parallel")),
        name=name,
    )(*args)
    return outs if cast is not None else outs[0]


def _full_source(k, vt, n):
    return [(k, pl.BlockSpec((n, LANES), lambda b, j: (b, 0))),
            (vt, pl.BlockSpec((LANES, n), lambda b, j: (0, b)))]


def _cache_source(k_all, vt_all, layer, past):
    return [(k_all, pl.BlockSpec((None, past, LANES), lambda b, j: (layer, b, 0))),
            (vt_all, pl.BlockSpec((None, LANES, past), lambda b, j: (layer, 0, b)))]


def _band_source(k, vt, n, tq):
    per = n // WINDOW
    step = tq // WINDOW
    prev = lambda b, j: b * per + jnp.maximum(step * j - 1, 0)
    nxt = lambda b, j: b * per + jnp.minimum(step * j + step, per - 1)
    mid = lambda b, j: b * (n // tq) + j
    return [(k, pl.BlockSpec((WINDOW, LANES), lambda b, j: (prev(b, j), 0))),
            (k, pl.BlockSpec((tq, LANES), lambda b, j: (mid(b, j), 0))),
            (k, pl.BlockSpec((WINDOW, LANES), lambda b, j: (nxt(b, j), 0))),
            (vt, pl.BlockSpec((LANES, WINDOW), lambda b, j: (0, prev(b, j)))),
            (vt, pl.BlockSpec((LANES, tq), lambda b, j: (0, mid(b, j)))),
            (vt, pl.BlockSpec((LANES, WINDOW), lambda b, j: (0, nxt(b, j))))]


def _ret_kernel(*refs, n, chunk, has_init, emit_state, bpb):
    r_ref, dec_ref, bd_ref = refs[:3]
    pos = 3
    if has_init:
        s0f_ref, s0b_ref = refs[pos:pos + 2]
        pos += 2
    o_ref = refs[pos]
    pos += 1
    if emit_state:
        sf_ref, sb_ref = refs[pos:pos + 2]
        pos += 2
    stf, stb, dsum, dec, cdec = refs[pos:pos + 5]
    nc = n // chunk
    n_slab = RET_HEADS // 2
    use_inter = has_init or nc > 1
    lane_lo = lax.broadcasted_iota(jnp.int32, (chunk, LANES), 1) < HEAD_DIM
    row_s = lax.broadcasted_iota(jnp.int32, (LANES, LANES), 0) < HEAD_DIM
    lane_s = lax.broadcasted_iota(jnp.int32, (LANES, LANES), 1) < HEAD_DIM
    blockdiag = row_s == lane_s

    @pl.when(pl.program_id(0) == 0)
    def _():
        x = dec_ref[...]
        lg = jnp.minimum(x, 0.0) - jnp.log(1.0 + jnp.exp(-jnp.abs(x)))
        rowc = lax.broadcasted_iota(jnp.int32, (chunk, LANES), 0).astype(F32)
        diff = (lax.broadcasted_iota(jnp.int32, (chunk, chunk), 0)
                - lax.broadcasted_iota(jnp.int32, (chunk, chunk), 1)).astype(F32)
        for hd in range(RET_HEADS):
            lf = lg[hd:hd + 1, 0:1]
            lb = lg[RET_HEADS + hd:RET_HEADS + hd + 1, 0:1]
            dsum[hd] = (jnp.where(diff >= 0, jnp.exp(jnp.maximum(diff, 0.0) * lf), 0.0)
                        + jnp.where(diff <= 0, jnp.exp(jnp.maximum(-diff, 0.0) * lb), 0.0))
        for sl in range(n_slab):
            f0, f1 = lg[2 * sl:2 * sl + 1], lg[2 * sl + 1:2 * sl + 2]
            b0 = lg[RET_HEADS + 2 * sl:RET_HEADS + 2 * sl + 1]
            b1 = lg[RET_HEADS + 2 * sl + 1:RET_HEADS + 2 * sl + 2]
            lf2 = jnp.where(lane_lo, f0, f1)
            lb2 = jnp.where(lane_lo, b0, b1)
            dec[0 + sl] = jnp.exp((chunk - 1.0 - rowc) * lf2)
            dec[2 + sl] = jnp.exp(rowc * lb2)
            dec[4 + sl] = jnp.exp((rowc + 1.0) * lf2)
            dec[6 + sl] = jnp.exp((chunk - rowc) * lb2)
            cdec[sl] = jnp.exp(chunk * jnp.where(row_s, f0, f1))
            cdec[n_slab + sl] = jnp.exp(chunk * jnp.where(row_s, b0, b1))

    lo, hi = slice(0, HEAD_DIM), slice(HEAD_DIM, LANES)
    for bi in range(bpb):
        for sl in range(n_slab):
            stf[bi, 0, sl] = jnp.zeros((LANES, LANES), F32)
            stb[bi, nc, sl] = jnp.zeros((LANES, LANES), F32)
            if has_init:
                stf[bi, 0, sl, lo, lo] = s0f_ref[bi, 2 * sl]
                stf[bi, 0, sl, hi, hi] = s0f_ref[bi, 2 * sl + 1]
                stb[bi, nc, sl, lo, lo] = s0b_ref[bi, 2 * sl]
                stb[bi, nc, sl, hi, hi] = s0b_ref[bi, 2 * sl + 1]

    def state_chain(bi, ci, sl):
        rows = _chunk_rows(bi * n, ci, chunk)
        yield
        k2 = r_ref[rows, 256 + sl * LANES:256 + (sl + 1) * LANES].astype(F32)
        kk = jnp.concatenate([(k2 * dec[0 + sl]).astype(BF16), (k2 * dec[2 + sl]).astype(BF16)], axis=1)
        yield
        u = _dot_tn(kk, r_ref[rows, 512 + sl * LANES:512 + (sl + 1) * LANES])
        yield
        stf[bi, ci + 1, sl] = jnp.where(blockdiag, u[:LANES], 0.0)
        stb[bi, ci, sl] = jnp.where(blockdiag, u[LANES:], 0.0)

    def output_chain(bi, ci, sl):
        rows = _chunk_rows(bi * n, ci, chunk)
        cols = lambda off: slice(off + sl * LANES, off + (sl + 1) * LANES)
        q2 = r_ref[rows, cols(0)]
        k2 = r_ref[rows, cols(256)]
        zero = jnp.zeros_like(q2)
        s_lo = _dot_nt(jnp.where(lane_lo, q2, zero), k2)
        s_hi = _dot_nt(jnp.where(lane_lo, zero, q2), k2)
        if use_inter:
            it = _dot(q2, jnp.concatenate([stf[bi, ci, sl], stb[bi, ci + 1, sl]], axis=1).astype(BF16))
        yield
        a_lo = (s_lo * dsum[2 * sl]).astype(BF16)
        a_hi = (s_hi * dsum[2 * sl + 1]).astype(BF16)
        yield
        v2 = r_ref[rows, cols(512)]
        o_lo, o_hi = _dot(a_lo, v2), _dot(a_hi, v2)
        yield
        o2 = jnp.where(lane_lo, o_lo, o_hi)
        if use_inter:
            o2 = o2 + it[:, :LANES] * dec[4 + sl] + it[:, LANES:] * dec[6 + sl]
        sq = (o2 * o2).astype(BF16)
        yield
        ss = _dot(sq, bd_ref[...])
        yield
        g2 = r_ref[rows, cols(768)].astype(F32)
        o_ref[rows, sl * LANES:(sl + 1) * LANES] = (o2 * lax.rsqrt(ss + EPS) * _silu(g2)).astype(BF16)

    def scan_states(bi):
        for sl in range(n_slab):
            for ci in range(nc):
                stf[bi, ci + 1, sl] = stf[bi, ci, sl] * cdec[sl] + stf[bi, ci + 1, sl]
            for ci in range(nc - 1, -1, -1):
                stb[bi, ci, sl] = stb[bi, ci + 1, sl] * cdec[n_slab + sl] + stb[bi, ci, sl]
            if emit_state:
                sf_ref[bi, 2 * sl] = stf[bi, nc, sl, lo, lo]
                sf_ref[bi, 2 * sl + 1] = stf[bi, nc, sl, hi, hi]
                sb_ref[bi, 2 * sl] = stb[bi, 0, sl, lo, lo]
                sb_ref[bi, 2 * sl + 1] = stb[bi, 0, sl, hi, hi]

    slabs = range(n_slab)
    if not use_inter:
        _run_skewed([state_chain(bi, 0, sl) for bi in range(bpb) for sl in slabs]
                    + [output_chain(bi, 0, sl) for bi in range(bpb) for sl in slabs])
        for bi in range(bpb):
            scan_states(bi)
    else:
        for bi in range(bpb):
            group = RET_CHUNK_GROUP if nc % RET_CHUNK_GROUP == 0 else 1

            def local_state(i, carry):
                _run_skewed([state_chain(bi, i * group + u, sl) for u in range(group) for sl in slabs])
                return carry

            def outputs(i, carry):
                _run_skewed([output_chain(bi, i * group + u, sl) for u in range(group) for sl in slabs])
                return carry

            lax.fori_loop(0, nc // group, local_state, 0)
            scan_states(bi)
            lax.fori_loop(0, nc // group, outputs, 0)


def _retention(r, dec_tab, bd, init, *, layer, bsz, n, emit_state, name, bpb=1):
    rw = r.shape[1]
    chunk = RET_CHUNK
    nc = n // chunk
    n_slab = RET_HEADS // 2
    has_init = init is not None
    in_specs = [pl.BlockSpec((bpb * n, rw), lambda b: (b, 0)),
                pl.BlockSpec((None, 8, LANES), lambda b: (layer, 0, 0)),
                pl.BlockSpec((LANES, LANES), lambda b: (0, 0))]
    args = [r, dec_tab, bd]
    if has_init:
        init_spec = pl.BlockSpec((bpb, None, RET_HEADS, HEAD_DIM, HEAD_DIM), lambda b: (b, layer, 0, 0, 0))
        in_specs += [init_spec, init_spec]
        args += list(init)
    ow = RET_HEADS * HEAD_DIM
    out_shape = [jax.ShapeDtypeStruct((bsz * n, ow), BF16)]
    out_specs = [pl.BlockSpec((bpb * n, ow), lambda b: (b, 0))]
    if emit_state:
        st_spec = pl.BlockSpec((bpb, RET_HEADS, HEAD_DIM, HEAD_DIM), lambda b: (b, 0, 0, 0))
        out_shape += [jax.ShapeDtypeStruct((bsz, RET_HEADS, HEAD_DIM, HEAD_DIM), F32)] * 2
        out_specs += [st_spec, st_spec]
    return pl.pallas_call(
        functools.partial(_ret_kernel, n=n, chunk=chunk, has_init=has_init, emit_state=emit_state, bpb=bpb),
        grid=(bsz // bpb,),
        in_specs=in_specs, out_specs=out_specs, out_shape=out_shape,
        scratch_shapes=[pltpu.VMEM((bpb, nc + 1, n_slab, LANES, LANES), F32),
                        pltpu.VMEM((bpb, nc + 1, n_slab, LANES, LANES), F32),
                        pltpu.VMEM((RET_HEADS, chunk, chunk), F32),
                        pltpu.VMEM((8, chunk, LANES), F32),
                        pltpu.VMEM((2 * n_slab, LANES, LANES), F32)],
        compiler_params=_cparams(("arbitrary",)),
        name=name,
    )(*args)


def _post_kernel(x_ref, og_ref, or_ref, ow_ref, mod_ref, wo_ref, wgu_ref, wd_ref,
                 gpost_ref, gpre2_ref, gpost2_ref, o_ref):
    m = mod_ref[0]
    gt1 = m[:, 2 * D_MODEL:3 * D_MODEL]
    sh2 = m[:, 3 * D_MODEL:4 * D_MODEL]
    sc2 = m[:, 4 * D_MODEL:5 * D_MODEL]
    gt2 = m[:, 5 * D_MODEL:6 * D_MODEL]
    sub = x_ref.shape[0] // POST_SUBTILES
    tiles = [slice(t * sub, (t + 1) * sub) for t in range(POST_SUBTILES)]

    def out_proj(rows):
        merged = jnp.concatenate([og_ref[rows, :], or_ref[rows, :], ow_ref[rows, :]], axis=1)
        return _dot(merged, wo_ref[...])

    def norms(rows, mixed):
        x1 = x_ref[rows, :] + gt1 * (_row_rms(mixed) * gpost_ref[...])
        return x1, (_row_rms(x1) * (gpre2_ref[...] * (1.0 + sc2)) + sh2).astype(BF16)

    def ffn(h2):
        ff = None
        for lo, hi in FF_CHUNKS:
            act = (_silu(_dot(h2, wgu_ref[:, lo:hi])) * _dot(h2, wgu_ref[:, D_FF + lo:D_FF + hi])).astype(BF16)
            part = _dot(act, wd_ref[lo:hi, :])
            ff = part if ff is None else ff + part
        return ff

    mixed = [out_proj(rows) for rows in tiles]
    x1s, ffs = [], []
    for rows, mx in zip(tiles, mixed):
        x1, h2 = norms(rows, mx)
        x1s.append(x1)
        ffs.append(ffn(h2))
    for rows, x1, ff in zip(tiles, x1s, ffs):
        o_ref[rows, :] = x1 + gt2 * (_row_rms(ff) * gpost2_ref[...])


def _post(x2d, og, orr, ow, mod, w_out_b, w_gu_b, w_down_b, gpost, gpre2, gpost2, *, layer, ctx, rows_per_batch):
    t_rows = x2d.shape[0]
    tm = TM_POST
    row = lambda i: (i, 0)
    lay3 = lambda i: (layer, 0, 0)
    gain = pl.BlockSpec((None, 1, D_MODEL), lay3)
    return pl.pallas_call(
        _post_kernel,
        grid=(t_rows // tm,),
        in_specs=[pl.BlockSpec((tm, D_MODEL), row),
                  pl.BlockSpec((tm, og.shape[1]), row),
                  pl.BlockSpec((tm, orr.shape[1]), row),
                  pl.BlockSpec((tm, ow.shape[1]), row),
                  _mod_spec(layer, ctx, rows_per_batch // tm),
                  _resident((D_MODEL, D_MODEL), lambda i: (0, 0)),
                  _resident((D_MODEL, 2 * D_FF), lambda i: (0, 0)),
                  _resident((D_FF, D_MODEL), lambda i: (0, 0)),
                  gain, gain, gain],
        out_specs=pl.BlockSpec((tm, D_MODEL), row),
        out_shape=jax.ShapeDtypeStruct((t_rows, D_MODEL), F32),
        compiler_params=_cparams(("parallel",)),
        name="post_mixer",
    )(x2d, og, orr, ow, mod, w_out_b, w_gu_b, w_down_b, gpost, gpre2, gpost2)


def _rope_tables(n):
    half = HEAD_DIM // 4
    pos = jnp.arange(n, dtype=jnp.int32)
    row = (pos // GRID_W).astype(F32)
    col = (pos % GRID_W).astype(F32)
    freqs = ROPE_BASE ** (-jnp.arange(half, dtype=F32) / half)
    ang_r = row[:, None] * freqs[None, :]
    ang_c = col[:, None] * freqs[None, :]
    ang = jnp.concatenate([ang_r, ang_r, ang_c, ang_c], axis=-1)
    second = (jnp.arange(HEAD_DIM) % (2 * half)) >= half
    cos = jnp.cos(ang)
    sin = jnp.sin(ang)
    sin_a = jnp.where(second[None, :], sin, 0.0)
    sin_b = jnp.where(second[None, :], 0.0, -sin)
    tile = lambda t: jnp.concatenate([t, t], axis=-1)
    return tile(cos), tile(sin_a), tile(sin_b)


def _cache_layouts(ck, cv):
    b, _, past = ck.shape[:3]
    k = jnp.transpose(ck.reshape(b, DEPTH, past, LANES), (1, 0, 2, 3)).reshape(DEPTH, b * past, LANES)
    vt = jnp.transpose(cv.reshape(b, DEPTH, past, LANES), (1, 3, 0, 2)).reshape(DEPTH, LANES, b * past)
    return k.astype(BF16), vt.astype(BF16)


def kernel(x_prompt, x_sample, cache_glob_k, cache_glob_v, state_ret_fwd, state_ret_bwd, cache_win_k, cache_win_v, c, c_ctx, w_mod, b_mod, g_pre_mix, g_post_mix, g_pre_ffn, g_post_ffn, w_in, g_q, g_k, ret_decay_fwd, ret_decay_bwd, win_sink, w_out, w_gate_up, w_down):
    bc, nc_, _ = x_prompt.shape
    bs, ns, _ = x_sample.shape
    past = cache_glob_k.shape[2]

    cond8 = jnp.concatenate([c, c_ctx[None, :], jnp.zeros((MOD_ROWS - bs - 1, D_MODEL), F32)], axis=0)
    mod = _modulation(cond8, w_mod, b_mod).reshape(DEPTH * MOD_ROWS, 1, 6 * D_MODEL)

    w_in_b, w_out_b, w_gu_b, w_down_b = (w[0].astype(BF16) for w in (w_in, w_out, w_gate_up, w_down))
    rope_tabs = _rope_tables(ns)
    seg = jnp.arange(LANES) // HEAD_DIM
    bd = jnp.where(seg[:, None] == seg[None, :], 1.0 / HEAD_DIM, 0.0).astype(BF16)
    gain3 = lambda g: g.reshape(DEPTH, 1, D_MODEL)
    gpre, gpost, gpre2, gpost2 = gain3(g_pre_mix), gain3(g_post_mix), gain3(g_pre_ffn), gain3(g_post_ffn)
    gq_t = jnp.tile(g_q, (1, 2)).reshape(DEPTH, 1, LANES)
    gk_t = jnp.tile(g_k, (1, 2)).reshape(DEPTH, 1, LANES)
    dec_tab = jnp.broadcast_to(jnp.concatenate([ret_decay_fwd, ret_decay_bwd], axis=1)[:, :, None],
                               (DEPTH, 2 * RET_HEADS, LANES))
    init = (state_ret_fwd, state_ret_bwd)
    cgk, cgvt = _cache_layouts(cache_glob_k, cache_glob_v)
    cwk, cwvt = _cache_layouts(cache_win_k, cache_win_v)

    y = x_prompt.reshape(bc * nc_, D_MODEL)
    z = x_sample.reshape(bs * ns, D_MODEL)
    new = [[] for _ in range(6)]
    for l in range(DEPTH):
        qg, kg, vgt, r, qw, kw, vwt, kg32, vg32, kw32, vw32 = _in_proj(
            y, mod, gpre, w_in_b, gq_t, gk_t, bd, None, layer=l, ctx=True, rows_per_batch=nc_)
        nxt = l + 1 if l + 1 < DEPTH else None
        cast = lambda w: None if nxt is None else (w, nxt)
        og = _attention(qg, _full_source(kg, vgt, nc_ * CTX_BPB), ("split",), None, l, cast=cast(w_in),
                        rows=nc_ * CTX_BPB, sub=nc_, stack=GROUP, n=nc_, name="attn_glob_ctx")
        ow = _attention(qw, _full_source(kw, vwt, nc_ * CTX_BPB), ("split",), win_sink, l, cast=cast(w_out),
                        rows=nc_ * CTX_BPB, sub=nc_, stack=GROUP, n=nc_, name="attn_win_ctx")
        if nxt is not None:
            (og, w_in_next), (ow, w_out_next) = og, ow
        orr, sf, sb = _retention(r, dec_tab, bd, None, layer=l, bsz=bc, n=nc_, emit_state=True, bpb=CTX_BPB,
                                 name="ret_ctx")
        y = _post(y, og, orr, ow, mod, w_out_b, w_gu_b, w_down_b, gpost, gpre2, gpost2,
                  layer=l, ctx=True, rows_per_batch=nc_)
        for lst, t in zip(new, (kg32, vg32, sf, sb, kw32, vw32)):
            lst.append(t)

        qg, kg, vgt, r, qw, kw, vwt = _in_proj(
            z, mod, gpre, w_in_b, gq_t, gk_t, bd, rope_tabs, layer=l, ctx=False, rows_per_batch=ns)
        og = _attention(qg, _full_source(kg, vgt, ns) + _cache_source(cgk, cgvt, l, past),
                        ("shared", "shared"), None, l, cast=cast(w_gate_up),
                        rows=TQ_GLOB, sub=TQ_GLOB, stack=1, n=ns, name="attn_glob_smp")
        ow = _attention(qw, _band_source(kw, vwt, ns, TQ_WIN) + _cache_source(cwk, cwvt, l, past),
                        ("band", "shared"), win_sink, l, cast=cast(w_down),
                        rows=TQ_WIN, sub=WINDOW, stack=GLOB_HEADS, n=ns, name="attn_win_smp")
        if nxt is not None:
            (og, w_gu_next), (ow, w_down_next) = og, ow
        (orr,) = _retention(r, dec_tab, bd, init, layer=l, bsz=bs, n=ns, emit_state=False, name="ret_smp")
        z = _post(z, og, orr, ow, mod, w_out_b, w_gu_b, w_down_b, gpost, gpre2, gpost2,
                  layer=l, ctx=False, rows_per_batch=ns)
        if nxt is not None:
            w_in_b, w_out_b, w_gu_b, w_down_b = w_in_next, w_out_next, w_gu_next, w_down_next

    kv5 = lambda ts: jnp.transpose(jnp.stack(ts, axis=1).reshape(bc, DEPTH, KV_HEADS, HEAD_DIM, nc_),
                                   (0, 1, 4, 2, 3))
    st5 = lambda ts: jnp.stack(ts, axis=1)
    return (y.reshape(bc, nc_, D_MODEL), z.reshape(bs, ns, D_MODEL),
            kv5(new[0]), kv5(new[1]), st5(new[2]), st5(new[3]), kv5(new[4]), kv5(new[5]))
```
